```python
import math
import jax, jax.numpy as jnp
from jax import lax
import numpy as np

D_MODEL = 1024
BATCH = 8
SEQ = 2048
DEPTH = 4
DEC_BATCH = 128
DEC_SEQ = 8
PAST_LEN = 16384
PAGE_SIZE = 128

N_MIXERS = 2
N_HGRN_LAYERS = (DEPTH + 1) // 2
N_SSM_LAYERS = DEPTH // 2
HGRN_EXPAND = 128
HGRN_HEADS = D_MODEL // HGRN_EXPAND
HGRN_DK = HGRN_EXPAND
HGRN_DV = D_MODEL // HGRN_HEADS
HGRN_FD = HGRN_HEADS * HGRN_DK
HGRN_ID = HGRN_HEADS * HGRN_DV
HGRN_CHUNK = 64
SSM_GROUP = 16
SSM_GROUPS = D_MODEL // SSM_GROUP
SSM_STATE = 64
D_FF = 2816
N_EXPERTS = 8
TOP_K = 2
D_FF_EXPERT = 2816
NORM_EPS = 1e-6

kernel_name = 'hgrn2_s5_hybrid_decoder_step'


def _rmsnorm(x, w):
    xf = x.astype(jnp.float32)
    y = xf * lax.rsqrt(jnp.mean(xf * xf, axis=-1, keepdims=True) + NORM_EPS)
    return (y * w.astype(jnp.float32)).astype(x.dtype)


def _modulation(c, w, b):
    m = jax.nn.silu(c) @ w + b
    return jnp.split(m[:, None, :], 6, axis=-1)


def _swiglu(h, w_in, w_out):
    g, u = jnp.split(h @ w_in, 2, axis=-1)
    return (jax.nn.silu(g) * u) @ w_out


def _moe_swiglu(h, w_router, b_router, w_in, w_out):
    logits = h.astype(jnp.float32) @ w_router.astype(jnp.float32) + b_router.astype(jnp.float32)
    vals, idx = lax.top_k(logits, TOP_K)
    probs = jax.nn.softmax(vals, axis=-1)
    gates = jnp.sum(jax.nn.one_hot(idx, N_EXPERTS, dtype=jnp.float32) * probs[..., None], axis=-2)
    out = jnp.zeros(h.shape, jnp.float32)
    for e in range(N_EXPERTS):
        out = out + gates[..., e:e + 1] * _swiglu(h, w_in[e], w_out[e]).astype(jnp.float32)
    return out.astype(h.dtype)


def _hgrn2_lower_bounds(lb_param):
    p = jax.nn.softmax(lb_param.astype(jnp.float32), axis=0)
    cs = jnp.cumsum(p, axis=0)
    return cs - cs[0:1]


def _gla_chunk_scan(q, k, v, logf, s0):
    bsz, seqlen, nh, dk = q.shape
    dv = v.shape[-1]
    csz = math.gcd(seqlen, HGRN_CHUNK)
    nch = seqlen // csz

    def to_chunks(t):
        return t.reshape(bsz, nch, csz, nh, t.shape[-1]).transpose(1, 0, 3, 2, 4)

    causal = jnp.tril(jnp.ones((csz, csz), dtype=bool))[:, :, None]

    def step(state, inp):
        qi, ki, vi, gi = inp
        b = jnp.cumsum(gi, axis=2)
        o_inter = jnp.einsum('bhtd,bhde->bhte', qi * jnp.exp(b), state)
        diff = b[:, :, :, None, :] - b[:, :, None, :, :]
        decay = jnp.exp(jnp.where(causal, diff, -jnp.inf))
        scores = jnp.einsum('bhtd,bhsd,bhtsd->bhts', qi, ki, decay)
        o = o_inter + jnp.einsum('bhts,bhse->bhte', scores, vi)
        b_last = b[:, :, -1:, :]
        new_state = (jnp.exp(b_last[:, :, 0, :])[..., None] * state
                     + jnp.einsum('bhsd,bhse->bhde', ki * jnp.exp(b_last - b), vi))
        return new_state, o

    s_fin, o = lax.scan(step, s0, (to_chunks(q), to_chunks(k), to_chunks(v), to_chunks(logf)))
    o = o.transpose(1, 0, 3, 2, 4).reshape(bsz, seqlen, nh, dv)
    return o, s_fin


def _hgrn2_mixer(h, s0, w_in, lb, gnorm_w, w_out):
    bsz, seqlen, _ = h.shape
    q, f, i, g = jnp.split(h @ w_in, [HGRN_FD, 2 * HGRN_FD, 2 * HGRN_FD + HGRN_ID], axis=-1)
    q = jax.nn.silu(q.astype(jnp.float32))
    logf = jnp.logaddexp(jnp.log(lb), jnp.log1p(-lb) + jax.nn.log_sigmoid(f.astype(jnp.float32)))
    k = -jnp.expm1(logf)

    def heads(t, d):
        return t.reshape(bsz, seqlen, HGRN_HEADS, d)

    o, s_new = _gla_chunk_scan(heads(q, HGRN_DK), heads(k, HGRN_DK), heads(i.astype(jnp.float32), HGRN_DV),
                               heads(logf, HGRN_DK), s0.astype(jnp.float32))
    o = _rmsnorm(o, gnorm_w) * jax.nn.silu(heads(g.astype(jnp.float32), HGRN_DV))
    return o.reshape(bsz, seqlen, HGRN_ID).astype(h.dtype) @ w_out, s_new


def _complex_affine_combine(e1, e2):
    a1r, a1i, b1r, b1i = e1
    a2r, a2i, b2r, b2i = e2
    return (a2r * a1r - a2i * a1i,
            a2r * a1i + a2i * a1r,
            a2r * b1r - a2i * b1i + b2r,
            a2r * b1i + a2i * b1r + b2i)


def _s5_mixer(h, x0_re, x0_im, a_re, a_im, log_dt, b_re, b_im, c_re, c_im, d_skip, w_glu):
    bsz, seqlen, _ = h.shape
    a_re, a_im, b_re, b_im, c_re, c_im = (t.astype(jnp.float32) for t in (a_re, a_im, b_re, b_im, c_re, c_im))
    u = h.astype(jnp.float32).reshape(bsz, seqlen, SSM_GROUPS, SSM_GROUP)
    dt = jnp.exp(log_dt.astype(jnp.float32))[:, None]
    mag = jnp.exp(a_re * dt)
    lam_re = mag * jnp.cos(a_im * dt)
    lam_im = mag * jnp.sin(a_im * dt)
    den = a_re * a_re + a_im * a_im
    coef_re = ((lam_re - 1.0) * a_re + lam_im * a_im) / den
    coef_im = (lam_im * a_re - (lam_re - 1.0) * a_im) / den
    bb_re = coef_re[..., None] * b_re - coef_im[..., None] * b_im
    bb_im = coef_re[..., None] * b_im + coef_im[..., None] * b_re
    bu_re = jnp.einsum('gpc,blgc->lbgp', bb_re, u)
    bu_im = jnp.einsum('gpc,blgc->lbgp', bb_im, u)
    x0_re = x0_re.astype(jnp.float32)
    x0_im = x0_im.astype(jnp.float32)
    bu_re = bu_re.at[0].add(lam_re * x0_re - lam_im * x0_im)
    bu_im = bu_im.at[0].add(lam_re * x0_im + lam_im * x0_re)
    shape_a = (seqlen, 1, SSM_GROUPS, SSM_STATE)
    lam_re_t = jnp.broadcast_to(lam_re, shape_a)
    lam_im_t = jnp.broadcast_to(lam_im, shape_a)
    _, _, xs_re, xs_im = lax.associative_scan(_complex_affine_combine, (lam_re_t, lam_im_t, bu_re, bu_im), axis=0)
    y = jnp.einsum('gcp,lbgp->blgc', c_re, xs_re) - jnp.einsum('gcp,lbgp->blgc', c_im, xs_im)
    y = (y + d_skip.astype(jnp.float32).reshape(SSM_GROUPS, SSM_GROUP) * u).reshape(bsz, seqlen, D_MODEL)
    z = jax.nn.gelu(y).astype(h.dtype)
    za, zb = jnp.split(z @ w_glu, 2, axis=-1)
    return za * jax.nn.sigmoid(zb), xs_re[-1], xs_im[-1]


def _trunk(x, c, s_hgrn, s_re, s_im, norm_w, ada_w, ada_b, hgrn_w_in, hgrn_lb, hgrn_gnorm, hgrn_w_out,
           ssm_a_re, ssm_a_im, ssm_log_dt, ssm_b_re, ssm_b_im, ssm_c_re, ssm_c_im, ssm_d, ssm_w_glu,
           ffn_w_in, ffn_w_out, moe_w_router, moe_b_router, moe_w_in, moe_w_out, final_norm_w):
    lower_bounds = _hgrn2_lower_bounds(hgrn_lb)
    new_hgrn, new_re, new_im = [], [], []
    for layer in range(DEPTH):
        slot = layer // N_MIXERS
        sh1, sc1, g1, sh2, sc2, g2 = _modulation(c, ada_w[layer], ada_b[layer])
        h = _rmsnorm(x, norm_w[layer, 0]) * (1.0 + sc1) + sh1
        if layer % N_MIXERS == 0:
            m, s_new = _hgrn2_mixer(h, s_hgrn[slot], hgrn_w_in[slot], lower_bounds[slot],
                                    hgrn_gnorm[slot], hgrn_w_out[slot])
            new_hgrn.append(s_new)
        else:
            m, xr, xi = _s5_mixer(h, s_re[slot], s_im[slot], ssm_a_re[slot], ssm_a_im[slot], ssm_log_dt[slot],
                                  ssm_b_re[slot], ssm_b_im[slot], ssm_c_re[slot], ssm_c_im[slot],
                                  ssm_d[slot], ssm_w_glu[slot])
            new_re.append(xr)
            new_im.append(xi)
        x = x + g1 * m
        h = _rmsnorm(x, norm_w[layer, 1]) * (1.0 + sc2) + sh2
        if layer % 2 == 0:
            f = _swiglu(h, ffn_w_in[slot], ffn_w_out[slot])
        else:
            f = _moe_swiglu(h, moe_w_router[slot], moe_b_router[slot], moe_w_in[slot], moe_w_out[slot])
        x = x + g2 * f
    return _rmsnorm(x, final_norm_w), jnp.stack(new_hgrn), jnp.stack(new_re), jnp.stack(new_im)


def setup_inputs(seed: int = 0) -> dict:
    key = jax.random.key(seed)
    ks = jax.random.split(key, 32)

    def nrm(k, shape, s):
        return jax.random.normal(k, shape, jnp.float32) * s

    D = D_MODEL
    g, p, sg = SSM_GROUPS, SSM_STATE, SSM_GROUP
    n_dense = N_HGRN_LAYERS
    n_moe = N_SSM_LAYERS
    return {
        'x_prompt': nrm(ks[0], (BATCH, SEQ, D), 1.0),
        'x_sample': nrm(ks[1], (DEC_BATCH, DEC_SEQ, D), 1.0),
        'state_hgrn': nrm(ks[2], (N_HGRN_LAYERS, DEC_BATCH, HGRN_HEADS, HGRN_DK, HGRN_DV), 0.5),
        'state_ssm_re': nrm(ks[3], (N_SSM_LAYERS, DEC_BATCH, g, p), 0.5),
        'state_ssm_im': nrm(ks[4], (N_SSM_LAYERS, DEC_BATCH, g, p), 0.5),
        'c_prompt': nrm(ks[5], (BATCH, D), 1.0),
        'c_sample': nrm(ks[6], (DEC_BATCH, D), 1.0),
        'norm_w': 1.0 + nrm(ks[7], (DEPTH, 2, D), 0.02),
        'ada_w': nrm(ks[8], (DEPTH, D, 6 * D), 0.5 * D ** -0.5),
        'ada_b': nrm(ks[9], (DEPTH, 6 * D), 0.02),
        'hgrn_w_in': nrm(ks[10], (N_HGRN_LAYERS, D, 2 * HGRN_FD + 2 * HGRN_ID), D ** -0.5),
        'hgrn_lb': 1.0 + nrm(ks[11], (N_HGRN_LAYERS, HGRN_FD), 0.1),
        'hgrn_gnorm': 1.0 + nrm(ks[12], (N_HGRN_LAYERS, HGRN_DV), 0.02),
        'hgrn_w_out': nrm(ks[13], (N_HGRN_LAYERS, HGRN_ID, D), HGRN_ID ** -0.5),
        'ssm_a_re': -0.5 + nrm(ks[14], (N_SSM_LAYERS, g, p), 0.01),
        'ssm_a_im': math.pi * jnp.arange(p, dtype=jnp.float32) + nrm(ks[15], (N_SSM_LAYERS, g, p), 0.01),
        'ssm_log_dt': jax.random.uniform(ks[16], (N_SSM_LAYERS, g), jnp.float32, math.log(1e-3), math.log(1e-1)),
        'ssm_b_re': nrm(ks[17], (N_SSM_LAYERS, g, p, sg), (2 * sg) ** -0.5),
        'ssm_b_im': nrm(ks[18], (N_SSM_LAYERS, g, p, sg), (2 * sg) ** -0.5),
        'ssm_c_re': nrm(ks[19], (N_SSM_LAYERS, g, sg, p), (2 * p) ** -0.5),
        'ssm_c_im': nrm(ks[20], (N_SSM_LAYERS, g, sg, p), (2 * p) ** -0.5),
        'ssm_d': nrm(ks[21], (N_SSM_LAYERS, D), 1.0),
        'ssm_w_glu': nrm(ks[22], (N_SSM_LAYERS, D, 2 * D), D ** -0.5),
        'ffn_w_in': nrm(ks[23], (n_dense, D, 2 * D_FF), D ** -0.5),
        'ffn_w_out': nrm(ks[24], (n_dense, D_FF, D), D_FF ** -0.5),
        'moe_w_router': nrm(ks[25], (n_moe, D, N_EXPERTS), D ** -0.5),
        'moe_b_router': nrm(ks[26], (n_moe, N_EXPERTS), 0.01),
        'moe_w_in': nrm(ks[27], (n_moe, N_EXPERTS, D, 2 * D_FF_EXPERT), D ** -0.5),
        'moe_w_out': nrm(ks[28], (n_moe, N_EXPERTS, D_FF_EXPERT, D), D_FF_EXPERT ** -0.5),
        'final_norm_w': 1.0 + nrm(ks[29], (D,), 0.02),
    }


def reference(x_prompt, x_sample, state_hgrn, state_ssm_re, state_ssm_im, c_prompt, c_sample,
              norm_w, ada_w, ada_b, hgrn_w_in, hgrn_lb, hgrn_gnorm, hgrn_w_out,
              ssm_a_re, ssm_a_im, ssm_log_dt, ssm_b_re, ssm_b_im, ssm_c_re, ssm_c_im, ssm_d, ssm_w_glu,
              ffn_w_in, ffn_w_out, moe_w_router, moe_b_router, moe_w_in, moe_w_out, final_norm_w):
    weights = (norm_w, ada_w, ada_b, hgrn_w_in, hgrn_lb, hgrn_gnorm, hgrn_w_out,
               ssm_a_re, ssm_a_im, ssm_log_dt, ssm_b_re, ssm_b_im, ssm_c_re, ssm_c_im, ssm_d, ssm_w_glu,
               ffn_w_in, ffn_w_out, moe_w_router, moe_b_router, moe_w_in, moe_w_out, final_norm_w)
    bsz = x_prompt.shape[0]
    zero_hgrn = jnp.zeros((N_HGRN_LAYERS, bsz, HGRN_HEADS, HGRN_DK, HGRN_DV), jnp.float32)
    zero_ssm = jnp.zeros((N_SSM_LAYERS, bsz, SSM_GROUPS, SSM_STATE), jnp.float32)
    y_prompt, hgrn_p, re_p, im_p = _trunk(x_prompt, c_prompt, zero_hgrn, zero_ssm, zero_ssm, *weights)
    y_sample, hgrn_s, re_s, im_s = _trunk(x_sample, c_sample, state_hgrn, state_ssm_re, state_ssm_im, *weights)
    return (y_prompt, y_sample, hgrn_p, re_p, im_p, hgrn_s, re_s, im_s)
```

```python
import functools
import math

import jax
import jax.numpy as jnp
from jax import lax
from jax.experimental import pallas as pl
from jax.experimental.pallas import tpu as pltpu

F32 = jnp.float32
BF16 = jnp.bfloat16

NORM_EPS = 1e-6
LANES = 128
SUBLANES = 8
ROW_TILE = 512
HEAD_DIM = 128
GLA_CHUNK = 64
SSM_GROUP = 16
SSM_LANE_CHUNK = 1024
VMEM_LIMIT = 56 * 1024 * 1024


def _params(*sem):
    return pltpu.CompilerParams(dimension_semantics=sem, vmem_limit_bytes=VMEM_LIMIT)


def _sigmoid(x):
    return 1.0 / (1.0 + jnp.exp(-x))


def _silu(x):
    return x * _sigmoid(x)


def _dot(a, b):
    return jnp.dot(a, b, preferred_element_type=F32)


def _norm_mod(x3, nw, sc, sh):
    ms = jnp.mean(x3 * x3, axis=-1, keepdims=True)
    y = x3 * lax.rsqrt(ms + NORM_EPS) * nw
    return y * (1.0 + sc[None]) + sh[None]


def _row_specs(tl, b, d):
    x_spec = pl.BlockSpec((tl, b, d), lambda i, *_: (i, 0, 0))

    def mod_spec(k):
        return pl.BlockSpec((b, d), lambda i, *_: (0, k))

    return x_spec, mod_spec


def _mod_kernel(c_ref, w_ref, b_ref, o_ref):
    c = c_ref[...]
    o_ref[0] = _dot(_silu(c).astype(BF16), w_ref[0].astype(BF16)) + b_ref[0]


def _modulation(c_all, ada_w, ada_b):
    depth, d, n = ada_w.shape
    rows = c_all.shape[0]
    tn = 1536
    assert n % tn == 0
    return pl.pallas_call(
        _mod_kernel,
        grid=(depth, n // tn),
        in_specs=[
            pl.BlockSpec((rows, d), lambda l, j: (0, 0)),
            pl.BlockSpec((1, d, tn), lambda l, j: (l, 0, j)),
            pl.BlockSpec((1, 1, tn), lambda l, j: (l, 0, j)),
        ],
        out_specs=pl.BlockSpec((1, rows, tn), lambda l, j: (l, 0, j)),
        out_shape=jax.ShapeDtypeStruct((depth, rows, n), F32),
        compiler_params=_params("arbitrary", "arbitrary"),
        name="adaln_modulation",
    )(c_all, ada_w, ada_b.reshape(depth, 1, n))


def _hgrn_in_kernel(x_ref, nw_ref, sh_ref, sc_ref, w_ref, la_ref, lc_ref, q_ref, lf_ref, v_ref, g_ref):
    tl, b, d = x_ref.shape
    h = _norm_mod(x_ref[...], nw_ref[...], sc_ref[...], sh_ref[...]).reshape(tl * b, d).astype(BF16)
    q_ref[...] = _silu(_dot(h, w_ref[:, 0:d])).astype(BF16).reshape(tl, b, d)
    f = _dot(h, w_ref[:, d : 2 * d])
    log_sig = jnp.minimum(f, 0.0) - jnp.log1p(jnp.exp(-jnp.abs(f)))
    c = lc_ref[...] + log_sig
    a = la_ref[...]
    lf = jnp.maximum(a, c) + jnp.log1p(jnp.exp(-jnp.abs(a - c)))
    lf_ref[...] = lf.reshape(tl, b, d)
    v_ref[...] = _dot(h, w_ref[:, 2 * d : 3 * d]).astype(BF16).reshape(tl, b, d)
    g_ref[...] = _silu(_dot(h, w_ref[:, 3 * d : 4 * d])).astype(BF16).reshape(tl, b, d)


def _hgrn_in(x3, mod, nw, w_in, log_lb, log1m_lb):
    l, b, d = x3.shape
    tl = min(ROW_TILE // b, l)
    assert l % tl == 0
    x_spec, mod_spec = _row_specs(tl, b, d)
    row = pl.BlockSpec((1, d), lambda i: (0, 0))
    return pl.pallas_call(
        _hgrn_in_kernel,
        grid=(l // tl,),
        in_specs=[x_spec, row, mod_spec(0), mod_spec(1), pl.BlockSpec((d, 4 * d), lambda i: (0, 0)), row, row],
        out_specs=[x_spec, x_spec, x_spec, x_spec],
        out_shape=[
            jax.ShapeDtypeStruct((l, b, d), BF16),
            jax.ShapeDtypeStruct((l, b, d), F32),
            jax.ShapeDtypeStruct((l, b, d), BF16),
            jax.ShapeDtypeStruct((l, b, d), BF16),
        ],
        compiler_params=_params("arbitrary"),
        name="hgrn_in_proj",
    )(x3, nw, mod, mod, w_in, log_lb, log1m_lb)


def _gla_head(q, lf, v, g, gw, s, c):
    nb = c // SUBLANES
    row = lax.broadcasted_iota(jnp.int32, (c, c), 0)
    col = lax.broadcasted_iota(jnp.int32, (c, c), 1)
    tri = (row >= col).astype(F32)
    b = jnp.dot(tri, lf, preferred_element_type=F32, precision=lax.Precision.HIGHEST)
    k = 1.0 - jnp.exp(lf)
    vf = v.astype(F32)

    o = _dot((q * jnp.exp(b)).astype(BF16), s.astype(BF16))

    b3 = b.reshape(nb, SUBLANES, HEAD_DIM)
    q3 = q.reshape(nb, SUBLANES, HEAD_DIM)
    k3 = k.reshape(nb, SUBLANES, HEAD_DIM)
    v3 = vf.reshape(nb, SUBLANES, HEAD_DIM)
    t_in = lax.broadcasted_iota(jnp.int32, (nb, SUBLANES, HEAD_DIM), 1)
    o_diag = jnp.zeros((nb, SUBLANES, HEAD_DIM), F32)
    for s_in in range(SUBLANES):
        diff = b3 - b3[:, s_in : s_in + 1, :]
        e = jnp.exp(jnp.where(t_in >= s_in, diff, -jnp.inf))
        w = jnp.sum(q3 * e * k3[:, s_in : s_in + 1, :], axis=-1, keepdims=True)
        o_diag = o_diag + w * v3[:, s_in : s_in + 1, :]
    o = o + o_diag.reshape(c, HEAD_DIM)

    if c > SUBLANES:
        t_row = lax.broadcasted_iota(jnp.int32, (c, HEAD_DIM), 0)
        a = jnp.zeros((c, c), F32)
        half = SUBLANES
        while half < c:
            span = 2 * half
            m = b.reshape(c // span, span, HEAD_DIM)[:, half - 1 : half, :]
            m = jnp.broadcast_to(m, (c // span, span, HEAD_DIM)).reshape(c, HEAD_DIM)
            right = (t_row & half) != 0
            qh = q * jnp.exp(jnp.where(right, b - m, -jnp.inf))
            kh = k * jnp.exp(jnp.where(right, -jnp.inf, m - b))
            sc = lax.dot_general(qh.astype(BF16), kh.astype(BF16), (((1,), (1,)), ((), ())),
                                 preferred_element_type=F32)
            shift = int(math.log2(span))
            a = a + jnp.where((row >> shift) == (col >> shift), sc, 0.0)
            half = span
        o = o + _dot(a.astype(BF16), v)

    b_last = b[c - 1 : c, :]
    kd = k * jnp.exp(b_last - b)
    pad = HEAD_DIM - c
    x = jnp.concatenate([kd, jnp.broadcast_to(jnp.exp(b_last), (pad, HEAD_DIM))], axis=0)
    xt = x.T
    v_pad = jnp.concatenate([vf, jnp.zeros((pad, HEAD_DIM), F32)], axis=0).astype(BF16)
    s_new = xt[:, c : c + 1] * s + _dot(xt.astype(BF16), v_pad)

    ms = jnp.mean(o * o, axis=-1, keepdims=True)
    out = o * lax.rsqrt(ms + NORM_EPS) * gw * g
    return out.astype(BF16), s_new


def _gla_kernel(*refs, c, heads, n_chunks, has_s0):
    if has_s0:
        q_ref, lf_ref, v_ref, g_ref, gw_ref, s0_ref, o_ref, sf_ref, s_scr = refs
    else:
        q_ref, lf_ref, v_ref, g_ref, gw_ref, o_ref, sf_ref, s_scr = refs
    step = pl.program_id(1)

    @pl.when(step == 0)
    def _():
        if has_s0:
            s_scr[...] = s0_ref[0]
        else:
            s_scr[...] = jnp.zeros_like(s_scr)

    gw = gw_ref[...]
    for h in range(heads):
        sl = slice(h * HEAD_DIM, (h + 1) * HEAD_DIM)
        out, s_new = _gla_head(q_ref[:, sl].astype(F32), lf_ref[:, sl], v_ref[:, sl],
                               g_ref[:, sl].astype(F32), gw, s_scr[h], c)
        o_ref[:, sl] = out
        s_scr[h] = s_new

    @pl.when(step == n_chunks - 1)
    def _():
        sf_ref[0] = s_scr[...]


def _gla(q3, lf3, v3, g3, gnorm_w, s0):
    l, b, d = q3.shape
    heads = d // HEAD_DIM
    c = math.gcd(l, GLA_CHUNK)
    n_chunks = l // c
    has_s0 = s0 is not None
    blk = pl.BlockSpec((c, d), lambda i, j: (j, i))
    s_spec = pl.BlockSpec((1, heads, HEAD_DIM, HEAD_DIM), lambda i, j: (i, 0, 0, 0))
    args = [q3.reshape(l, b * d), lf3.reshape(l, b * d), v3.reshape(l, b * d), g3.reshape(l, b * d), gnorm_w]
    in_specs = [blk, blk, blk, blk, pl.BlockSpec((1, HEAD_DIM), lambda i, j: (0, 0))]
    if has_s0:
        args.append(s0)
        in_specs.append(s_spec)
    o, s_fin = pl.pallas_call(
        functools.partial(_gla_kernel, c=c, heads=heads, n_chunks=n_chunks, has_s0=has_s0),
        grid=(b, n_chunks),
        in_specs=in_specs,
        out_specs=[blk, s_spec],
        out_shape=[
            jax.ShapeDtypeStruct((l, b * d), BF16),
            jax.ShapeDtypeStruct((b, heads, HEAD_DIM, HEAD_DIM), F32),
        ],
        scratch_shapes=[pltpu.VMEM((heads, HEAD_DIM, HEAD_DIM), F32)],
        compiler_params=_params("arbitrary", "arbitrary"),
        name="hgrn_gla_scan",
    )(*args)
    return o.reshape(l, b, d), s_fin


def _proj_res_kernel(a_ref, w_ref, x_ref, gt_ref, o_ref):
    tl, b, d = x_ref.shape
    y = _dot(a_ref[...].reshape(tl * b, a_ref.shape[-1]), w_ref[...]).reshape(tl, b, d)
    o_ref[...] = x_ref[...] + gt_ref[...][None] * y


def _proj_res(a3, w, x3, mod, gate_idx):
    l, b, d = x3.shape
    k = a3.shape[-1]
    tl = min(ROW_TILE // b, l)
    x_spec, mod_spec = _row_specs(tl, b, d)
    return pl.pallas_call(
        _proj_res_kernel,
        grid=(l // tl,),
        in_specs=[pl.BlockSpec((tl, b, k), lambda i: (i, 0, 0)), pl.BlockSpec((k, d), lambda i: (0, 0)),
                  x_spec, mod_spec(gate_idx)],
        out_specs=x_spec,
        out_shape=jax.ShapeDtypeStruct((l, b, d), F32),
        compiler_params=_params("arbitrary"),
        name="proj_residual",
    )(a3, w, x3, mod)


def _ff_tile(d_ff):
    best = LANES
    for t in range(LANES, 1408 + 1, LANES):
        if d_ff % t == 0:
            best = t
    return best


def _ffn_kernel(x_ref, nw_ref, sh_ref, sc_ref, gt_ref, wg_ref, wu_ref, wo_ref, o_ref, h_scr, acc_scr, *, nf):
    tl, b, d = x_ref.shape
    j = pl.program_id(1)

    @pl.when(j == 0)
    def _():
        h = _norm_mod(x_ref[...], nw_ref[...], sc_ref[...], sh_ref[...])
        h_scr[...] = h.reshape(tl * b, d).astype(BF16)
        acc_scr[...] = jnp.zeros_like(acc_scr)

    h = h_scr[...]
    act = (_silu(_dot(h, wg_ref[...])) * _dot(h, wu_ref[...])).astype(BF16)
    acc_scr[...] += _dot(act, wo_ref[...])

    @pl.when(j == nf - 1)
    def _():
        o_ref[...] = x_ref[...] + gt_ref[...][None] * acc_scr[...].reshape(tl, b, d)


def _ffn(x3, mod, nw, w_in, w_out):
    l, b, d = x3.shape
    d_ff = w_out.shape[0]
    tf = _ff_tile(d_ff)
    nf = d_ff // tf
    tl = min(ROW_TILE // b, l)
    x_spec, mod_spec = _row_specs(tl, b, d)
    return pl.pallas_call(
        functools.partial(_ffn_kernel, nf=nf),
        grid=(l // tl, nf),
        in_specs=[
            x_spec, pl.BlockSpec((1, d), lambda i, j: (0, 0)), mod_spec(3), mod_spec(4), mod_spec(5),
            pl.BlockSpec((d, tf), lambda i, j: (0, j)),
            pl.BlockSpec((d, tf), lambda i, j: (0, nf + j)),
            pl.BlockSpec((tf, d), lambda i, j: (j, 0)),
        ],
        out_specs=x_spec,
        out_shape=jax.ShapeDtypeStruct((l, b, d), F32),
        scratch_shapes=[pltpu.VMEM((tl * b, d), BF16), pltpu.VMEM((tl * b, d), F32)],
        compiler_params=_params("arbitrary", "arbitrary"),
        name="dense_swiglu",
    )(x3, nw, mod, mod, mod, w_in, w_in, w_out)


def _top2_gates(logits):
    lane = lax.broadcasted_iota(jnp.int32, logits.shape, 1)
    m0 = jnp.max(logits, axis=-1, keepdims=True)
    i0 = jnp.min(jnp.where(logits == m0, lane, LANES), axis=-1, keepdims=True)
    rest = jnp.where(lane == i0, -jnp.inf, logits)
    m1 = jnp.max(rest, axis=-1, keepdims=True)
    i1 = jnp.min(jnp.where(rest == m1, lane, LANES), axis=-1, keepdims=True)
    e1 = jnp.exp(m1 - m0)
    p0 = 1.0 / (1.0 + e1)
    p1 = e1 / (1.0 + e1)
    return jnp.where(lane == i0, p0, 0.0) + jnp.where(lane == i1, p1, 0.0)


def _moe_kernel(x_ref, nw_ref, sh_ref, sc_ref, gt_ref, wr_ref, br_ref, wg_ref, wu_ref, wo_ref, o_ref,
                h_scr, gates_scr, acc_scr, *, n_exp, nf):
    tl, b, d = x_ref.shape
    e = pl.program_id(1)
    j = pl.program_id(2)

    @pl.when((e == 0) & (j == 0))
    def _():
        h = _norm_mod(x_ref[...], nw_ref[...], sc_ref[...], sh_ref[...]).reshape(tl * b, d)
        h_scr[...] = h.astype(BF16)
        logits = jnp.dot(h, wr_ref[...], preferred_element_type=F32, precision=lax.Precision.HIGHEST)
        gates_scr[...] = _top2_gates(logits + br_ref[...])
        acc_scr[...] = jnp.zeros_like(acc_scr)

    h = h_scr[...]
    act = (_silu(_dot(h, wg_ref[0])) * _dot(h, wu_ref[0])).astype(BF16)
    lane = lax.broadcasted_iota(jnp.int32, gates_scr.shape, 1)
    gate = jnp.sum(jnp.where(lane == e, gates_scr[...], 0.0), axis=-1, keepdims=True)
    acc_scr[...] += gate * _dot(act, wo_ref[0])

    @pl.when((e == n_exp - 1) & (j == nf - 1))
    def _():
        o_ref[...] = x_ref[...] + gt_ref[...][None] * acc_scr[...].reshape(tl, b, d)


def _moe(x3, mod, nw, w_router, b_router, w_in, w_out):
    l, b, d = x3.shape
    n_exp, d_ff, _ = w_out.shape
    tf = _ff_tile(d_ff)
    nf = d_ff // tf
    tl = min(ROW_TILE // b, l)
    x_spec, mod_spec = _row_specs(tl, b, d)
    wr = jnp.zeros((d, LANES), F32).at[:, :n_exp].set(w_router)
    br = jnp.full((1, LANES), -jnp.inf, F32).at[0, :n_exp].set(b_router)
    return pl.pallas_call(
        functools.partial(_moe_kernel, n_exp=n_exp, nf=nf),
        grid=(l // tl, n_exp, nf),
        in_specs=[
            x_spec, pl.BlockSpec((1, d), lambda i, e, j: (0, 0)), mod_spec(3), mod_spec(4), mod_spec(5),
            pl.BlockSpec((d, LANES), lambda i, e, j: (0, 0)),
            pl.BlockSpec((1, LANES), lambda i, e, j: (0, 0)),
            pl.BlockSpec((1, d, tf), lambda i, e, j: (e, 0, j)),
            pl.BlockSpec((1, d, tf), lambda i, e, j: (e, 0, nf + j)),
            pl.BlockSpec((1, tf, d), lambda i, e, j: (e, j, 0)),
        ],
        out_specs=x_spec,
        out_shape=jax.ShapeDtypeStruct((l, b, d), F32),
        scratch_shapes=[pltpu.VMEM((tl * b, d), BF16), pltpu.VMEM((tl * b, LANES), F32),
                        pltpu.VMEM((tl * b, d), F32)],
        compiler_params=_params("arbitrary", "arbitrary", "arbitrary"),
        name="moe_swiglu",
    )(x3, nw, mod, mod, mod, wr, br, w_in, w_in, w_out)


def _gelu_tanh(x):
    return 0.5 * x * (1.0 + jnp.tanh(math.sqrt(2.0 / math.pi) * (x + 0.044715 * (x * x * x))))


def _s5_kernel(x_ref, nw_ref, sh_ref, sc_ref, gt_ref, bre_ref, bim_ref, cre_ref, cim_ref, lr_ref, li_ref,
               dsk_ref, wglu_ref, x0r_ref, x0i_ref, o_ref, fr_ref, fi_ref,
               h_scr, bur, bui, y_scr, sr, si, *, n_steps):
    tc, bb, d = x_ref.shape
    rows = tc * bb
    n_lane_chunks = bre_ref.shape[0]
    in_chunk = bre_ref.shape[1]
    half_lanes = SSM_LANE_CHUNK // 2
    t = pl.program_id(1)

    @pl.when(t == 0)
    def _():
        sr[...] = x0r_ref[...]
        si[...] = x0i_ref[...]

    x3 = x_ref[...]
    h_scr[...] = _norm_mod(x3, nw_ref[...], sc_ref[...], sh_ref[...]).reshape(rows, d)

    for n in range(n_lane_chunks):
        u = h_scr[:, n * in_chunk : (n + 1) * in_chunk].astype(BF16)
        bur[...] = _dot(u, bre_ref[n])
        bui[...] = _dot(u, bim_ref[n])
        for hf in range(2):
            ls = slice(hf * half_lanes, (hf + 1) * half_lanes)
            g0 = n * SSM_LANE_CHUNK + hf * half_lanes
            lam_r = jnp.broadcast_to(lr_ref[:, g0 : g0 + half_lanes], (SUBLANES, half_lanes))
            lam_i = jnp.broadcast_to(li_ref[:, g0 : g0 + half_lanes], (SUBLANES, half_lanes))

            def batch_tile(bt, _):
                r0 = pl.multiple_of(bt * SUBLANES, SUBLANES)

                def step(l, carry):
                    xr, xi = carry
                    row = pl.multiple_of(l * bb + r0, SUBLANES)
                    nr = lam_r * xr - lam_i * xi + bur[pl.ds(row, SUBLANES), ls]
                    ni = lam_r * xi + lam_i * xr + bui[pl.ds(row, SUBLANES), ls]
                    bur[pl.ds(row, SUBLANES), ls] = nr
                    bui[pl.ds(row, SUBLANES), ls] = ni
                    return nr, ni

                xr, xi = lax.fori_loop(
                    0, tc, step,
                    (sr[pl.ds(r0, SUBLANES), g0 : g0 + half_lanes], si[pl.ds(r0, SUBLANES), g0 : g0 + half_lanes]),
                    unroll=8)
                sr[pl.ds(r0, SUBLANES), g0 : g0 + half_lanes] = xr
                si[pl.ds(r0, SUBLANES), g0 : g0 + half_lanes] = xi
                return 0

            lax.fori_loop(0, bb // SUBLANES, batch_tile, 0)
        y_scr[:, n * in_chunk : (n + 1) * in_chunk] = (
            _dot(bur[...].astype(BF16), cre_ref[n]) - _dot(bui[...].astype(BF16), cim_ref[n]))

    y = y_scr[...] + dsk_ref[...] * h_scr[...]
    zz = _dot(_gelu_tanh(y).astype(BF16), wglu_ref[...])
    m = zz[:, :d] * _sigmoid(zz[:, d:])
    o_ref[...] = x3 + gt_ref[...][None] * m.reshape(tc, bb, d)

    @pl.when(t == n_steps - 1)
    def _():
        fr_ref[...] = sr[...]
        fi_ref[...] = si[...]


def _s5(x3, mod, nw, x0_re, x0_im, lam_re, lam_im, b_re_blk, b_im_blk, c_re_blk, c_im_blk, d_skip, w_glu):
    l, b, d = x3.shape
    n_state = lam_re.shape[-1]
    if b * l <= ROW_TILE:
        bb, tc = b, l
    elif b <= ROW_TILE // SUBLANES:
        bb, tc = b, ROW_TILE // b
    else:
        tc = l
        bb = ROW_TILE // l
    assert l % tc == 0 and b % bb == 0 and bb % SUBLANES == 0
    n_steps = l // tc
    rows = tc * bb
    x_spec = pl.BlockSpec((tc, bb, d), lambda i, t: (t, i, 0))

    def mod_spec(k):
        return pl.BlockSpec((bb, d), lambda i, t: (i, k))

    def full(a):
        nd = a.ndim
        return pl.BlockSpec(a.shape, lambda i, t: (0,) * nd)

    st_spec = pl.BlockSpec((bb, n_state), lambda i, t: (i, 0))
    out, fr, fi = pl.pallas_call(
        functools.partial(_s5_kernel, n_steps=n_steps),
        grid=(b // bb, n_steps),
        in_specs=[x_spec, full(nw), mod_spec(0), mod_spec(1), mod_spec(2), full(b_re_blk), full(b_im_blk),
                  full(c_re_blk), full(c_im_blk), full(lam_re), full(lam_im), full(d_skip), full(w_glu),
                  st_spec, st_spec],
        out_specs=[x_spec, st_spec, st_spec],
        out_shape=[jax.ShapeDtypeStruct((l, b, d), F32), jax.ShapeDtypeStruct((b, n_state), F32),
                   jax.ShapeDtypeStruct((b, n_state), F32)],
        scratch_shapes=[pltpu.VMEM((rows, d), F32), pltpu.VMEM((rows, SSM_LANE_CHUNK), F32),
                        pltpu.VMEM((rows, SSM_LANE_CHUNK), F32), pltpu.VMEM((rows, d), F32),
                        pltpu.VMEM((bb, n_state), F32), pltpu.VMEM((bb, n_state), F32)],
        compiler_params=_params("arbitrary", "arbitrary"),
        name="s5_mixer",
    )(x3, nw, mod, mod, mod, b_re_blk, b_im_blk, c_re_blk, c_im_blk, lam_re, lam_im, d_skip, w_glu, x0_re, x0_im)
    return out, fr, fi


def _s5_discretize(a_re, a_im, log_dt, b_re, b_im, c_re, c_im):
    g, p = a_re.shape
    dt = jnp.exp(log_dt)[:, None]
    mag = jnp.exp(a_re * dt)
    lam_re = mag * jnp.cos(a_im * dt)
    lam_im = mag * jnp.sin(a_im * dt)
    den = a_re * a_re + a_im * a_im
    coef_re = ((lam_re - 1.0) * a_re + lam_im * a_im) / den
    coef_im = (lam_im * a_re - (lam_re - 1.0) * a_im) / den
    bb_re = coef_re[..., None] * b_re - coef_im[..., None] * b_im
    bb_im = coef_re[..., None] * b_im + coef_im[..., None] * b_re
    gpb = SSM_LANE_CHUNK // p
    nblk = g // gpb
    eye = jnp.eye(gpb, dtype=F32)

    def in_map(bb):
        t = bb.reshape(nblk, gpb, p, SSM_GROUP)
        return jnp.einsum("ngpc,gh->ngchp", t, eye).reshape(nblk, gpb * SSM_GROUP, gpb * p).astype(BF16)

    def out_map(c):
        t = c.reshape(nblk, gpb, SSM_GROUP, p)
        return jnp.einsum("ngcp,gh->ngphc", t, eye).reshape(nblk, gpb * p, gpb * SSM_GROUP).astype(BF16)

    return (lam_re.reshape(1, g * p), lam_im.reshape(1, g * p), in_map(bb_re), in_map(bb_im),
            out_map(c_re), out_map(c_im))


def _final_norm_kernel(x_ref, w_ref, o_ref):
    x = x_ref[...]
    ms = jnp.mean(x * x, axis=-1, keepdims=True)
    o_ref[...] = x * lax.rsqrt(ms + NORM_EPS) * w_ref[...]


def _final_norm(x3, w):
    l, b, d = x3.shape
    tl = min(ROW_TILE // b, l)
    x_spec, _ = _row_specs(tl, b, d)
    return pl.pallas_call(
        _final_norm_kernel,
        grid=(l // tl,),
        in_specs=[x_spec, pl.BlockSpec((1, d), lambda i: (0, 0))],
        out_specs=x_spec,
        out_shape=jax.ShapeDtypeStruct((l, b, d), F32),
        compiler_params=_params("arbitrary"),
        name="final_norm",
    )(x3, w)


def _trunk(x3, mods, s_hgrn, s_re, s_im, p):
    depth = p["norm_w"].shape[0]
    d = x3.shape[-1]
    new_hgrn, new_re, new_im = [], [], []
    for layer in range(depth):
        slot = layer // 2
        mod = mods[layer]
        nw1 = p["norm_w"][layer, 0].reshape(1, d)
        nw2 = p["norm_w"][layer, 1].reshape(1, d)
        if layer % 2 == 0:
            q, lf, v, g = _hgrn_in(x3, mod, nw1, p["hgrn_w_in"][slot], p["log_lb"][slot], p["log1m_lb"][slot])
            o, s_new = _gla(q, lf, v, g, p["hgrn_gnorm"][slot].reshape(1, HEAD_DIM),
                            None if s_hgrn is None else s_hgrn[slot])
            new_hgrn.append(s_new)
            x3 = _proj_res(o, p["hgrn_w_out"][slot], x3, mod, 2)
            x3 = _ffn(x3, mod, nw2, p["ffn_w_in"][slot], p["ffn_w_out"][slot])
        else:
            b = x3.shape[1]
            n_state = p["s5"][slot][0].shape[-1]
            if s_re is None:
                x0r = x0i = jnp.zeros((b, n_state), F32)
            else:
                x0r = s_re[slot].reshape(b, n_state)
                x0i = s_im[slot].reshape(b, n_state)
            lam_re, lam_im, bre, bim, cre, cim = p["s5"][slot]
            x3, fr, fi = _s5(x3, mod, nw1, x0r, x0i, lam_re, lam_im, bre, bim, cre, cim,
                             p["ssm_d"][slot].reshape(1, d), p["ssm_w_glu"][slot])
            new_re.append(fr)
            new_im.append(fi)
            x3 = _moe(x3, mod, nw2, p["moe_w_router"][slot], p["moe_b_router"][slot],
                      p["moe_w_in"][slot], p["moe_w_out"][slot])
    y = _final_norm(x3, p["final_norm_w"].reshape(1, d))
    return y, jnp.stack(new_hgrn), jnp.stack(new_re), jnp.stack(new_im)


def kernel(x_prompt, x_sample, state_hgrn, state_ssm_re, state_ssm_im, c_prompt, c_sample, norm_w, ada_w, ada_b,
           hgrn_w_in, hgrn_lb, hgrn_gnorm, hgrn_w_out, ssm_a_re, ssm_a_im, ssm_log_dt, ssm_b_re, ssm_b_im,
           ssm_c_re, ssm_c_im, ssm_d, ssm_w_glu, ffn_w_in, ffn_w_out, moe_w_router, moe_b_router, moe_w_in,
           moe_w_out, final_norm_w):
    bp, lp, d = x_prompt.shape
    bs, ls, _ = x_sample.shape
    n_ssm, g, n_p = ssm_a_re.shape

    pr = jax.nn.softmax(hgrn_lb, axis=0)
    cs = jnp.cumsum(pr, axis=0)
    lb = cs - cs[0:1]
    p = {
        "norm_w": norm_w,
        "log_lb": jnp.log(lb)[:, None, :],
        "log1m_lb": jnp.log1p(-lb)[:, None, :],
        "hgrn_w_in": hgrn_w_in.astype(BF16),
        "hgrn_gnorm": hgrn_gnorm,
        "hgrn_w_out": hgrn_w_out.astype(BF16),
        "s5": [_s5_discretize(ssm_a_re[i], ssm_a_im[i], ssm_log_dt[i], ssm_b_re[i], ssm_b_im[i],
                              ssm_c_re[i], ssm_c_im[i]) for i in range(n_ssm)],
        "ssm_d": ssm_d,
        "ssm_w_glu": ssm_w_glu.astype(BF16),
        "ffn_w_in": ffn_w_in.astype(BF16),
        "ffn_w_out": ffn_w_out.astype(BF16),
        "moe_w_router": moe_w_router,
        "moe_b_router": moe_b_router,
        "moe_w_in": moe_w_in.astype(BF16),
        "moe_w_out": moe_w_out.astype(BF16),
        "final_norm_w": final_norm_w,
    }

    mods = _modulation(jnp.concatenate([c_prompt, c_sample], axis=0), ada_w, ada_b)
    mods_p = mods[:, :bp]
    mods_s = mods[:, bp:]

    xp = jnp.transpose(x_prompt, (1, 0, 2))
    xs = jnp.transpose(x_sample, (1, 0, 2))
    yp, hgrn_p, re_p, im_p = _trunk(xp, mods_p, None, None, None, p)
    ys, hgrn_s, re_s, im_s = _trunk(xs, mods_s, state_hgrn, state_ssm_re, state_ssm_im, p)

    def ssm_shape(a, b):
        return a.reshape(n_ssm, b, g, n_p)

    return (jnp.transpose(yp, (1, 0, 2)), jnp.transpose(ys, (1, 0, 2)),
            hgrn_p, ssm_shape(re_p, bp), ssm_shape(im_p, bp),
            hgrn_s, ssm_shape(re_s, bs), ssm_shape(im_s, bs))
```

```python
import functools
import math

import jax
import jax.numpy as jnp
from jax import lax
from jax.experimental import pallas as pl
from jax.experimental.pallas import tpu as pltpu

F32 = jnp.float32
BF16 = jnp.bfloat16

NORM_EPS = 1e-6
LANES = 128
SUBLANES = 8
ROW_TILE = 512
HEAD_DIM = 128
GLA_CHUNK = 64
SSM_GROUP = 16
SSM_LANE_CHUNK = 1024
MOE_TILE = 768
VMEM_LIMIT = 56 * 1024 * 1024


def _params(*sem):
    return pltpu.CompilerParams(dimension_semantics=sem, vmem_limit_bytes=VMEM_LIMIT)


def _sigmoid(x):
    return 1.0 / (1.0 + jnp.exp(-x))


def _silu(x):
    return x * _sigmoid(x)


def _dot(a, b):
    return jnp.dot(a, b, preferred_element_type=F32)


def _norm_mod(x3, nw, sc, sh):
    ms = jnp.mean(x3 * x3, axis=-1, keepdims=True)
    y = x3 * lax.rsqrt(ms + NORM_EPS) * nw
    return y * (1.0 + sc[None]) + sh[None]


def _row_specs(tl, b, d):
    x_spec = pl.BlockSpec((tl, b, d), lambda i, *_: (i, 0, 0))

    def mod_spec(k):
        return pl.BlockSpec((b, d), lambda i, *_: (0, k))

    return x_spec, mod_spec


def _mod_kernel(c_ref, w_ref, b_ref, o_ref):
    c = c_ref[...]
    o_ref[0] = _dot(_silu(c).astype(BF16), w_ref[0].astype(BF16)) + b_ref[0]


def _modulation(c_all, ada_w, ada_b):
    depth, d, n = ada_w.shape
    rows = c_all.shape[0]
    tn = 1536
    assert n % tn == 0
    return pl.pallas_call(
        _mod_kernel,
        grid=(depth, n // tn),
        in_specs=[
            pl.BlockSpec((rows, d), lambda l, j: (0, 0)),
            pl.BlockSpec((1, d, tn), lambda l, j: (l, 0, j)),
            pl.BlockSpec((1, 1, tn), lambda l, j: (l, 0, j)),
        ],
        out_specs=pl.BlockSpec((1, rows, tn), lambda l, j: (l, 0, j)),
        out_shape=jax.ShapeDtypeStruct((depth, rows, n), F32),
        compiler_params=_params("arbitrary", "arbitrary"),
        name="adaln_modulation",
    )(c_all, ada_w, ada_b.reshape(depth, 1, n))


def _hgrn_in_kernel(x_ref, nw_ref, sh_ref, sc_ref, w_ref, la_ref, lc_ref, q_ref, lf_ref, v_ref, g_ref):
    tl, b, d = x_ref.shape
    h = _norm_mod(x_ref[...], nw_ref[...], sc_ref[...], sh_ref[...]).reshape(tl * b, d).astype(BF16)
    q_ref[...] = _silu(_dot(h, w_ref[:, 0:d])).astype(BF16).reshape(tl, b, d)
    f = _dot(h, w_ref[:, d : 2 * d])
    log_sig = jnp.minimum(f, 0.0) - jnp.log1p(jnp.exp(-jnp.abs(f)))
    c = lc_ref[...] + log_sig
    a = la_ref[...]
    lf = jnp.maximum(a, c) + jnp.log1p(jnp.exp(-jnp.abs(a - c)))
    lf_ref[...] = lf.reshape(tl, b, d)
    v_ref[...] = _dot(h, w_ref[:, 2 * d : 3 * d]).astype(BF16).reshape(tl, b, d)
    g_ref[...] = _silu(_dot(h, w_ref[:, 3 * d : 4 * d])).astype(BF16).reshape(tl, b, d)


def _hgrn_in(x3, mod, nw, w_in, slot, log_lb, log1m_lb):
    l, b, d = x3.shape
    tl = min(ROW_TILE // b, l)
    assert l % tl == 0
    x_spec, mod_spec = _row_specs(tl, b, d)
    row = pl.BlockSpec((1, d), lambda i: (0, 0))
    return pl.pallas_call(
        _hgrn_in_kernel,
        grid=(l // tl,),
        in_specs=[x_spec, row, mod_spec(0), mod_spec(1),
                  pl.BlockSpec((None, d, 4 * d), lambda i: (slot, 0, 0)), row, row],
        out_specs=[x_spec, x_spec, x_spec, x_spec],
        out_shape=[
            jax.ShapeDtypeStruct((l, b, d), BF16),
            jax.ShapeDtypeStruct((l, b, d), F32),
            jax.ShapeDtypeStruct((l, b, d), BF16),
            jax.ShapeDtypeStruct((l, b, d), BF16),
        ],
        compiler_params=_params("arbitrary"),
        name="hgrn_in_proj",
    )(x3, nw, mod, mod, w_in, log_lb, log1m_lb)


def _gla_head(q, lf, v, g, gw, s, c):
    nb = c // SUBLANES
    row = lax.broadcasted_iota(jnp.int32, (c, c), 0)
    col = lax.broadcasted_iota(jnp.int32, (c, c), 1)
    tri = (row >= col).astype(F32)
    b = jnp.dot(tri, lf, preferred_element_type=F32, precision=lax.Precision.HIGHEST)
    k = 1.0 - jnp.exp(lf)
    vf = v.astype(F32)

    o = _dot((q * jnp.exp(b)).astype(BF16), s.astype(BF16))

    b3 = b.reshape(nb, SUBLANES, HEAD_DIM)
    q3 = q.reshape(nb, SUBLANES, HEAD_DIM)
    k3 = k.reshape(nb, SUBLANES, HEAD_DIM)
    v3 = vf.reshape(nb, SUBLANES, HEAD_DIM)
    t_in = lax.broadcasted_iota(jnp.int32, (nb, SUBLANES, HEAD_DIM), 1)
    o_diag = jnp.zeros((nb, SUBLANES, HEAD_DIM), F32)
    for s_in in range(SUBLANES):
        diff = b3 - b3[:, s_in : s_in + 1, :]
        e = jnp.exp(jnp.where(t_in >= s_in, diff, -jnp.inf))
        w = jnp.sum(q3 * e * k3[:, s_in : s_in + 1, :], axis=-1, keepdims=True)
        o_diag = o_diag + w * v3[:, s_in : s_in + 1, :]
    o = o + o_diag.reshape(c, HEAD_DIM)

    if c > SUBLANES:
        t_row = lax.broadcasted_iota(jnp.int32, (c, HEAD_DIM), 0)
        a = jnp.zeros((c, c), F32)
        half = SUBLANES
        while half < c:
            span = 2 * half
            m = b.reshape(c // span, span, HEAD_DIM)[:, half - 1 : half, :]
            m = jnp.broadcast_to(m, (c // span, span, HEAD_DIM)).reshape(c, HEAD_DIM)
            right = (t_row & half) != 0
            qh = q * jnp.exp(jnp.where(right, b - m, -jnp.inf))
            kh = k * jnp.exp(jnp.where(right, -jnp.inf, m - b))
            sc = lax.dot_general(qh.astype(BF16), kh.astype(BF16), (((1,), (1,)), ((), ())),
                                 preferred_element_type=F32)
            shift = int(math.log2(span))
            a = a + jnp.where((row >> shift) == (col >> shift), sc, 0.0)
            half = span
        o = o + _dot(a.astype(BF16), v)

    b_last = b[c - 1 : c, :]
    kd = k * jnp.exp(b_last - b)
    pad = HEAD_DIM - c
    x = jnp.concatenate([kd, jnp.broadcast_to(jnp.exp(b_last), (pad, HEAD_DIM))], axis=0)
    xt = x.T
    v_pad = jnp.concatenate([vf, jnp.zeros((pad, HEAD_DIM), F32)], axis=0).astype(BF16)
    s_new = xt[:, c : c + 1] * s + _dot(xt.astype(BF16), v_pad)

    ms = jnp.mean(o * o, axis=-1, keepdims=True)
    out = o * lax.rsqrt(ms + NORM_EPS) * gw * g
    return out.astype(BF16), s_new


def _gla_kernel(*refs, c, heads, n_chunks, has_s0, has_prev):
    q_ref, lf_ref, v_ref, g_ref, gw_ref = refs[:5]
    s0_ref = refs[5] if has_s0 else None
    o_ref, sf_ref, s_scr = refs[-3:]
    step = pl.program_id(1)

    @pl.when(step == 0)
    def _():
        if has_s0:
            s_scr[...] = s0_ref[0]
        else:
            s_scr[...] = jnp.zeros_like(s_scr)

    gw = gw_ref[...]
    for h in range(heads):
        sl = slice(h * HEAD_DIM, (h + 1) * HEAD_DIM)
        out, s_new = _gla_head(q_ref[:, sl].astype(F32), lf_ref[:, sl], v_ref[:, sl],
                               g_ref[:, sl].astype(F32), gw, s_scr[h], c)
        o_ref[:, sl] = out
        s_scr[h] = s_new

    @pl.when(step == n_chunks - 1)
    def _():
        sf_ref[0] = s_scr[...]


def _gla(q3, lf3, v3, g3, gnorm_w, s0_all, s_prev, slot, n_slots):
    l, b, d = q3.shape
    heads = d // HEAD_DIM
    c = math.gcd(l, GLA_CHUNK)
    n_chunks = l // c
    has_s0 = s0_all is not None
    has_prev = s_prev is not None
    blk = pl.BlockSpec((c, d), lambda i, j: (j, i))
    s_spec = pl.BlockSpec((None, 1, heads, HEAD_DIM, HEAD_DIM), lambda i, j: (slot, i, 0, 0, 0))
    args = [q3.reshape(l, b * d), lf3.reshape(l, b * d), v3.reshape(l, b * d), g3.reshape(l, b * d), gnorm_w]
    in_specs = [blk, blk, blk, blk, pl.BlockSpec((1, HEAD_DIM), lambda i, j: (0, 0))]
    if has_s0:
        args.append(s0_all)
        in_specs.append(s_spec)
    if has_prev:
        args.append(s_prev)
        in_specs.append(pl.BlockSpec(memory_space=pl.ANY))
    o, s_fin = pl.pallas_call(
        functools.partial(_gla_kernel, c=c, heads=heads, n_chunks=n_chunks, has_s0=has_s0, has_prev=has_prev),
        grid=(b, n_chunks),
        in_specs=in_specs,
        out_specs=[blk, s_spec],
        out_shape=[
            jax.ShapeDtypeStruct((l, b * d), BF16),
            jax.ShapeDtypeStruct((n_slots, b, heads, HEAD_DIM, HEAD_DIM), F32),
        ],
        input_output_aliases={len(args) - 1: 1} if has_prev else {},
        scratch_shapes=[pltpu.VMEM((heads, HEAD_DIM, HEAD_DIM), F32)],
        compiler_params=_params("arbitrary", "arbitrary"),
        name="hgrn_gla_scan",
    )(*args)
    return o.reshape(l, b, d), s_fin


def _proj_res_kernel(a_ref, w_ref, x_ref, gt_ref, o_ref):
    tl, b, d = x_ref.shape
    y = _dot(a_ref[...].reshape(tl * b, a_ref.shape[-1]), w_ref[...]).reshape(tl, b, d)
    o_ref[...] = x_ref[...] + gt_ref[...][None] * y


def _proj_res(a3, w, slot, x3, mod, gate_idx):
    l, b, d = x3.shape
    k = a3.shape[-1]
    tl = min(ROW_TILE // b, l)
    x_spec, mod_spec = _row_specs(tl, b, d)
    return pl.pallas_call(
        _proj_res_kernel,
        grid=(l // tl,),
        in_specs=[pl.BlockSpec((tl, b, k), lambda i: (i, 0, 0)), pl.BlockSpec((None, k, d), lambda i: (slot, 0, 0)),
                  x_spec, mod_spec(gate_idx)],
        out_specs=x_spec,
        out_shape=jax.ShapeDtypeStruct((l, b, d), F32),
        compiler_params=_params("arbitrary"),
        name="proj_residual",
    )(a3, w, x3, mod)


def _ff_tile(d_ff):
    best = LANES
    for t in range(LANES, 1408 + 1, LANES):
        if d_ff % t == 0:
            best = t
    return best


def _ffn_kernel(x_ref, nw_ref, sh_ref, sc_ref, gt_ref, wg_ref, wu_ref, wo_ref, o_ref, h_scr, acc_scr, *, nf):
    tl, b, d = x_ref.shape
    j = pl.program_id(1)

    @pl.when(j == 0)
    def _():
        h = _norm_mod(x_ref[...], nw_ref[...], sc_ref[...], sh_ref[...])
        h_scr[...] = h.reshape(tl * b, d).astype(BF16)
        acc_scr[...] = jnp.zeros_like(acc_scr)

    h = h_scr[...]
    act = (_silu(_dot(h, wg_ref[...])) * _dot(h, wu_ref[...])).astype(BF16)
    acc_scr[...] += _dot(act, wo_ref[...])

    @pl.when(j == nf - 1)
    def _():
        o_ref[...] = x_ref[...] + gt_ref[...][None] * acc_scr[...].reshape(tl, b, d)


def _ffn(x3, mod, nw, w_in, w_out, slot):
    l, b, d = x3.shape
    d_ff = w_out.shape[-2]
    tf = _ff_tile(d_ff)
    nf = d_ff // tf
    tl = min(ROW_TILE // b, l)
    x_spec, mod_spec = _row_specs(tl, b, d)
    return pl.pallas_call(
        functools.partial(_ffn_kernel, nf=nf),
        grid=(l // tl, nf),
        in_specs=[
            x_spec, pl.BlockSpec((1, d), lambda i, j: (0, 0)), mod_spec(3), mod_spec(4), mod_spec(5),
            pl.BlockSpec((None, d, tf), lambda i, j: (slot, 0, j)),
            pl.BlockSpec((None, d, tf), lambda i, j: (slot, 0, nf + j)),
            pl.BlockSpec((None, tf, d), lambda i, j: (slot, j, 0)),
        ],
        out_specs=x_spec,
        out_shape=jax.ShapeDtypeStruct((l, b, d), F32),
        scratch_shapes=[pltpu.VMEM((tl * b, d), BF16), pltpu.VMEM((tl * b, d), F32)],
        compiler_params=_params("arbitrary", "arbitrary"),
        name="dense_swiglu",
    )(x3, nw, mod, mod, mod, w_in, w_in, w_out)


def _route_kernel(x_ref, nw_ref, sh_ref, sc_ref, wr_ref, br_ref, cnt0_ref, h_ref, pos_ref, prob_ref, cnt_ref,
                  cnt_scr, *, cap, n_steps):
    tl, b, d = x_ref.shape
    rows = tl * b
    step = pl.program_id(0)

    @pl.when(step == 0)
    def _():
        cnt_scr[...] = cnt0_ref[...]

    h3 = _norm_mod(x_ref[...], nw_ref[...], sc_ref[...], sh_ref[...])
    h_ref[...] = h3
    logits = jnp.dot(h3.reshape(rows, d), wr_ref[...], preferred_element_type=F32,
                     precision=lax.Precision.HIGHEST) + br_ref[...]
    lane = lax.broadcasted_iota(jnp.int32, (rows, LANES), 1)
    m0 = jnp.max(logits, axis=-1, keepdims=True)
    i0 = jnp.min(jnp.where(logits == m0, lane, LANES), axis=-1, keepdims=True)
    rest = jnp.where(lane == i0, -jnp.inf, logits)
    m1 = jnp.max(rest, axis=-1, keepdims=True)
    i1 = jnp.min(jnp.where(rest == m1, lane, LANES), axis=-1, keepdims=True)
    e1 = jnp.exp(m1 - m0)
    p0 = 1.0 / (1.0 + e1)
    p1 = e1 / (1.0 + e1)
    hit0 = lane == i0
    hit1 = lane == i1
    onehot = jnp.where(hit0 | hit1, 1.0, 0.0)
    r_i = lax.broadcasted_iota(jnp.int32, (rows, rows), 0)
    c_i = lax.broadcasted_iota(jnp.int32, (rows, rows), 1)
    before = _dot((r_i > c_i).astype(BF16), onehot.astype(BF16)) + cnt_scr[...]
    rank0 = jnp.sum(jnp.where(hit0, before, 0.0), axis=-1, keepdims=True).astype(jnp.int32)
    rank1 = jnp.sum(jnp.where(hit1, before, 0.0), axis=-1, keepdims=True).astype(jnp.int32)
    pos_ref[...] = jnp.where(lane == 0, i0 * cap + rank0, jnp.where(lane == 1, i1 * cap + rank1, 0))
    prob_ref[...] = jnp.where(lane == 0, p0, jnp.where(lane == 1, p1, 0.0))
    cnt_scr[...] += jnp.sum(onehot, axis=0, keepdims=True)

    @pl.when(step == n_steps - 1)
    def _():
        cnt_ref[...] = cnt_scr[...]


def _route(x3, mod, nw, wr, br, cnt0, cap):
    l, b, d = x3.shape
    tl = min(ROW_TILE // b, l)
    n_steps = l // tl
    x_spec, mod_spec = _row_specs(tl, b, d)
    row = pl.BlockSpec((1, LANES), lambda i: (0, 0))
    lanes_spec = pl.BlockSpec((tl * b, LANES), lambda i: (i, 0))
    return pl.pallas_call(
        functools.partial(_route_kernel, cap=cap, n_steps=n_steps),
        grid=(n_steps,),
        in_specs=[x_spec, pl.BlockSpec((1, d), lambda i: (0, 0)), mod_spec(3), mod_spec(4),
                  pl.BlockSpec((d, LANES), lambda i: (0, 0)), row, row],
        out_specs=[x_spec, lanes_spec, lanes_spec, row],
        out_shape=[jax.ShapeDtypeStruct((l, b, d), F32), jax.ShapeDtypeStruct((l * b, LANES), jnp.int32),
                   jax.ShapeDtypeStruct((l * b, LANES), F32), jax.ShapeDtypeStruct((1, LANES), F32)],
        scratch_shapes=[pltpu.VMEM((1, LANES), F32)],
        compiler_params=_params("arbitrary"),
        name="moe_route",
    )(x3, nw, mod, mod, wr, br, cnt0)


def _row_copy(src, src_row, dst, dst_row, sem):
    return pltpu.make_async_copy(src.at[pl.ds(src_row, 1)], dst.at[pl.ds(dst_row, 1)], sem)


def _scatter_kernel(cnt_ref, pos_ref, h_ref, *rest, rows, cap, n_steps, fill):
    if fill:
        _, xg_ref, zero_row, sem = rest
    else:
        xg_ref, zero_row, sem = rest

    def issue(r, carry):
        _row_copy(h_ref, r, xg_ref, pos_ref[0, 0, 2 * r], sem).start()
        _row_copy(h_ref, r, xg_ref, pos_ref[0, 0, 2 * r + 1], sem).start()
        return carry

    lax.fori_loop(0, rows, issue, 0)
    for _ in range(2):
        pltpu.make_async_copy(h_ref, xg_ref.at[pl.ds(0, rows)], sem).wait()

    if fill:
        @pl.when(pl.program_id(0) == n_steps - 1)
        def _():
            zero_row[...] = jnp.zeros_like(zero_row)
            for e in range(cnt_ref.shape[0]):
                cnt = cnt_ref[e]
                n_pad = lax.rem(MOE_TILE - lax.rem(cnt, MOE_TILE), MOE_TILE)

                def put(r, carry, base=e * cap + cnt):
                    _row_copy(zero_row, 0, xg_ref, base + r, sem).start()
                    return carry

                def done(r, carry):
                    _row_copy(zero_row, 0, xg_ref, 0, sem).wait()
                    return carry

                lax.fori_loop(0, n_pad, put, 0)
                lax.fori_loop(0, n_pad, done, 0)


def _scatter(counts, pos, h2, xg, n_slots, cap):
    t, d = h2.shape
    rows = min(ROW_TILE, t)
    n_steps = t // rows
    fill = xg is not None
    in_specs = [pl.BlockSpec((1, 1, 2 * rows), lambda i, c: (i, 0, 0), memory_space=pltpu.SMEM),
                pl.BlockSpec((rows, d), lambda i, c: (i, 0))]
    args = [counts, pos, h2]
    if fill:
        in_specs.append(pl.BlockSpec(memory_space=pl.ANY))
        args.append(xg)
    return pl.pallas_call(
        functools.partial(_scatter_kernel, rows=rows, cap=cap, n_steps=n_steps, fill=fill),
        grid_spec=pltpu.PrefetchScalarGridSpec(
            num_scalar_prefetch=1, grid=(n_steps,), in_specs=in_specs,
            out_specs=pl.BlockSpec(memory_space=pl.ANY),
            scratch_shapes=[pltpu.VMEM((1, d), F32), pltpu.SemaphoreType.DMA(())]),
        out_shape=jax.ShapeDtypeStruct((n_slots, d), F32),
        input_output_aliases={3: 0} if fill else {},
        compiler_params=_params("arbitrary"),
        name="moe_scatter",
    )(*args)


def _gmm_kernel(blk_ref, exp_ref, used_ref, x_ref, wg_ref, wu_ref, wo_ref, o_ref, xb_scr):
    i = pl.program_id(0)
    j = pl.program_id(1)

    @pl.when(i < used_ref[0])
    def _():
        @pl.when(j == 0)
        def _():
            xb_scr[...] = x_ref[...].astype(BF16)

        h = xb_scr[...]
        act = (_silu(_dot(h, wg_ref[...])) * _dot(h, wu_ref[...])).astype(BF16)
        y = _dot(act, wo_ref[...])

        @pl.when(j == 0)
        def _():
            o_ref[...] = y

        @pl.when(j > 0)
        def _():
            o_ref[...] += y


def _gmm(tile_blk, tile_exp, n_used, xg, w_in, w_out, slot):
    n_slots, d = xg.shape
    d_ff = w_out.shape[-2]
    tf = _ff_tile(d_ff)
    nf = d_ff // tf
    n_tiles = tile_blk.shape[0]

    def ff(i, j, used):
        return jnp.where(i < used[0], j, nf - 1)

    row_spec = pl.BlockSpec((MOE_TILE, d), lambda i, j, blk, ex, used: (blk[i], 0))
    return pl.pallas_call(
        _gmm_kernel,
        grid_spec=pltpu.PrefetchScalarGridSpec(
            num_scalar_prefetch=3, grid=(n_tiles, nf),
            in_specs=[
                row_spec,
                pl.BlockSpec((None, None, d, tf), lambda i, j, blk, ex, used: (slot, ex[i], 0, ff(i, j, used))),
                pl.BlockSpec((None, None, d, tf),
                             lambda i, j, blk, ex, used: (slot, ex[i], 0, nf + ff(i, j, used))),
                pl.BlockSpec((None, None, tf, d), lambda i, j, blk, ex, used: (slot, ex[i], ff(i, j, used), 0)),
            ],
            out_specs=row_spec,
            scratch_shapes=[pltpu.VMEM((MOE_TILE, d), BF16)]),
        out_shape=jax.ShapeDtypeStruct((n_slots, d), F32),
        compiler_params=_params("arbitrary", "arbitrary"),
        name="moe_grouped_swiglu",
    )(tile_blk, tile_exp, n_used, xg, w_in, w_in, w_out)


def _combine_kernel(pos_ref, x_ref, gt_ref, prob_ref, y_ref, o_ref, buf0, buf1, sem):
    tl, b, d = x_ref.shape
    rows = tl * b

    def issue(r, carry):
        _row_copy(y_ref, pos_ref[0, 0, 2 * r], buf0, r, sem).start()
        _row_copy(y_ref, pos_ref[0, 0, 2 * r + 1], buf1, r, sem).start()
        return carry

    lax.fori_loop(0, rows, issue, 0)
    for buf in (buf0, buf1):
        pltpu.make_async_copy(y_ref.at[pl.ds(0, rows)], buf, sem).wait()
    prob = prob_ref[...]
    f = prob[:, 0:1] * buf0[...] + prob[:, 1:2] * buf1[...]
    o_ref[...] = x_ref[...] + gt_ref[...][None] * f.reshape(tl, b, d)


def _combine(pos, x3, mod, prob, y):
    l, b, d = x3.shape
    tl = min(ROW_TILE // b, l)
    rows = tl * b
    x_spec, mod_spec = _row_specs(tl, b, d)
    return pl.pallas_call(
        _combine_kernel,
        grid=(l // tl,),
        in_specs=[pl.BlockSpec((1, 1, 2 * rows), lambda i: (i, 0, 0), memory_space=pltpu.SMEM),
                  x_spec, mod_spec(5), pl.BlockSpec((rows, LANES), lambda i: (i, 0)),
                  pl.BlockSpec(memory_space=pl.ANY)],
        out_specs=x_spec,
        out_shape=jax.ShapeDtypeStruct((l, b, d), F32),
        scratch_shapes=[pltpu.VMEM((rows, d), F32), pltpu.VMEM((rows, d), F32), pltpu.SemaphoreType.DMA(())],
        compiler_params=_params("arbitrary"),
        name="moe_combine",
    )(pos, x3, mod, prob, y)


def _moe(xs, mods, nw, w_router, b_router, w_in, w_out, slot):
    d = xs[0].shape[-1]
    n_exp = w_in.shape[1]
    t_total = sum(x.shape[0] * x.shape[1] for x in xs)
    cap = -(-t_total // MOE_TILE) * MOE_TILE
    n_slots = n_exp * cap
    n_tiles = -(-(2 * t_total + n_exp * (MOE_TILE - 1)) // MOE_TILE)
    wr = jnp.zeros((d, LANES), F32).at[:, :n_exp].set(w_router)
    br = jnp.full((1, LANES), -jnp.inf, F32).at[0, :n_exp].set(b_router)

    cnt = jnp.zeros((1, LANES), F32)
    routed = []
    for x3, mod in zip(xs, mods):
        h3, pos, prob, cnt = _route(x3, mod, nw, wr, br, cnt, cap)
        rows = min(ROW_TILE, pos.shape[0])
        routed.append((h3, pos[:, :2].reshape(pos.shape[0] // rows, 1, 2 * rows), prob))
    counts = cnt[0, :n_exp].astype(jnp.int32)

    xg = None
    for h3, pos, _ in routed:
        xg = _scatter(counts, pos, h3.reshape(-1, d), xg, n_slots, cap)

    tiles = (counts + (MOE_TILE - 1)) // MOE_TILE
    ends = jnp.cumsum(tiles)
    n_used = ends[-1:]
    idx = jnp.minimum(jnp.arange(n_tiles, dtype=jnp.int32), n_used - 1)
    tile_exp = jnp.sum((idx[:, None] >= ends[None, :]).astype(jnp.int32), axis=1)
    tile_blk = tile_exp * (cap // MOE_TILE) + idx - (ends - tiles)[tile_exp]
    y = _gmm(tile_blk.astype(jnp.int32), tile_exp, n_used.astype(jnp.int32), xg, w_in, w_out, slot)

    return [_combine(pos, x3, mod, prob, y) for x3, mod, (_, pos, prob) in zip(xs, mods, routed)]


def _gelu_tanh(x):
    return 0.5 * x * (1.0 + jnp.tanh(math.sqrt(2.0 / math.pi) * (x + 0.044715 * (x * x * x))))


def _s5_kernel(x_ref, nw_ref, sh_ref, sc_ref, gt_ref, bre_ref, bim_ref, cre_ref, cim_ref, lr_ref, li_ref,
               dsk_ref, wglu_ref, x0r_ref, x0i_ref, o_ref, fr_ref, fi_ref,
               h_scr, bur, bui, y_scr, sr, si, *, n_steps):
    tc, bb, d = x_ref.shape
    rows = tc * bb
    n_lane_chunks = bre_ref.shape[0]
    in_chunk = bre_ref.shape[1]
    half_lanes = SSM_LANE_CHUNK // 2
    t = pl.program_id(1)

    @pl.when(t == 0)
    def _():
        sr[...] = x0r_ref[...]
        si[...] = x0i_ref[...]

    x3 = x_ref[...]
    h_scr[...] = _norm_mod(x3, nw_ref[...], sc_ref[...], sh_ref[...]).reshape(rows, d)

    for n in range(n_lane_chunks):
        u = h_scr[:, n * in_chunk : (n + 1) * in_chunk].astype(BF16)
        bur[...] = _dot(u, bre_ref[n])
        bui[...] = _dot(u, bim_ref[n])
        for hf in range(2):
            ls = slice(hf * half_lanes, (hf + 1) * half_lanes)
            g0 = n * SSM_LANE_CHUNK + hf * half_lanes
            lam_r = jnp.broadcast_to(lr_ref[:, g0 : g0 + half_lanes], (SUBLANES, half_lanes))
            lam_i = jnp.broadcast_to(li_ref[:, g0 : g0 + half_lanes], (SUBLANES, half_lanes))

            def batch_tile(bt, _):
                r0 = pl.multiple_of(bt * SUBLANES, SUBLANES)

                def step(l, carry):
                    xr, xi = carry
                    row = pl.multiple_of(l * bb + r0, SUBLANES)
                    nr = lam_r * xr - lam_i * xi + bur[pl.ds(row, SUBLANES), ls]
                    ni = lam_r * xi + lam_i * xr + bui[pl.ds(row, SUBLANES), ls]
                    bur[pl.ds(row, SUBLANES), ls] = nr
                    bui[pl.ds(row, SUBLANES), ls] = ni
                    return nr, ni

                xr, xi = lax.fori_loop(
                    0, tc, step,
                    (sr[pl.ds(r0, SUBLANES), g0 : g0 + half_lanes], si[pl.ds(r0, SUBLANES), g0 : g0 + half_lanes]),
                    unroll=8)
                sr[pl.ds(r0, SUBLANES), g0 : g0 + half_lanes] = xr
                si[pl.ds(r0, SUBLANES), g0 : g0 + half_lanes] = xi
                return 0

            lax.fori_loop(0, bb // SUBLANES, batch_tile, 0)
        y_scr[:, n * in_chunk : (n + 1) * in_chunk] = (
            _dot(bur[...].astype(BF16), cre_ref[n]) - _dot(bui[...].astype(BF16), cim_ref[n]))

    y = y_scr[...] + dsk_ref[...] * h_scr[...]
    zz = _dot(_gelu_tanh(y).astype(BF16), wglu_ref[...])
    m = zz[:, :d] * _sigmoid(zz[:, d:])
    o_ref[...] = x3 + gt_ref[...][None] * m.reshape(tc, bb, d)

    @pl.when(t == n_steps - 1)
    def _():
        fr_ref[...] = sr[...]
        fi_ref[...] = si[...]


def _s5(x3, mod, nw, x0_re, x0_im, lam_re, lam_im, b_re_blk, b_im_blk, c_re_blk, c_im_blk, d_skip, w_glu, slot):
    l, b, d = x3.shape
    n_state = lam_re.shape[-1]
    if b * l <= ROW_TILE:
        bb, tc = b, l
    elif b <= ROW_TILE // SUBLANES:
        bb, tc = b, ROW_TILE // b
    else:
        tc = l
        bb = ROW_TILE // l
    assert l % tc == 0 and b % bb == 0 and bb % SUBLANES == 0
    n_steps = l // tc
    rows = tc * bb
    x_spec = pl.BlockSpec((tc, bb, d), lambda i, t: (t, i, 0))

    def mod_spec(k):
        return pl.BlockSpec((bb, d), lambda i, t: (i, k))

    def full(a):
        nd = a.ndim
        return pl.BlockSpec(a.shape, lambda i, t: (0,) * nd)

    st_spec = pl.BlockSpec((bb, n_state), lambda i, t: (i, 0))
    out, fr, fi = pl.pallas_call(
        functools.partial(_s5_kernel, n_steps=n_steps),
        grid=(b // bb, n_steps),
        in_specs=[x_spec, full(nw), mod_spec(0), mod_spec(1), mod_spec(2), full(b_re_blk), full(b_im_blk),
                  full(c_re_blk), full(c_im_blk), full(lam_re), full(lam_im), full(d_skip),
                  pl.BlockSpec((None,) + w_glu.shape[1:], lambda i, t: (slot, 0, 0)),
                  st_spec, st_spec],
        out_specs=[x_spec, st_spec, st_spec],
        out_shape=[jax.ShapeDtypeStruct((l, b, d), F32), jax.ShapeDtypeStruct((b, n_state), F32),
                   jax.ShapeDtypeStruct((b, n_state), F32)],
        scratch_shapes=[pltpu.VMEM((rows, d), F32), pltpu.VMEM((rows, SSM_LANE_CHUNK), F32),
                        pltpu.VMEM((rows, SSM_LANE_CHUNK), F32), pltpu.VMEM((rows, d), F32),
                        pltpu.VMEM((bb, n_state), F32), pltpu.VMEM((bb, n_state), F32)],
        compiler_params=_params("arbitrary", "arbitrary"),
        name="s5_mixer",
    )(x3, nw, mod, mod, mod, b_re_blk, b_im_blk, c_re_blk, c_im_blk, lam_re, lam_im, d_skip, w_glu, x0_re, x0_im)
    return out, fr, fi


def _s5_discretize(a_re, a_im, log_dt, b_re, b_im, c_re, c_im):
    g, p = a_re.shape
    dt = jnp.exp(log_dt)[:, None]
    mag = jnp.exp(a_re * dt)
    lam_re = mag * jnp.cos(a_im * dt)
    lam_im = mag * jnp.sin(a_im * dt)
    den = a_re * a_re + a_im * a_im
    coef_re = ((lam_re - 1.0) * a_re + lam_im * a_im) / den
    coef_im = (lam_im * a_re - (lam_re - 1.0) * a_im) / den
    bb_re = coef_re[..., None] * b_re - coef_im[..., None] * b_im
    bb_im = coef_re[..., None] * b_im + coef_im[..., None] * b_re
    gpb = SSM_LANE_CHUNK // p
    nblk = g // gpb
    eye = jnp.eye(gpb, dtype=F32)

    def in_map(bb):
        t = bb.reshape(nblk, gpb, p, SSM_GROUP)
        return jnp.einsum("ngpc,gh->ngchp", t, eye).reshape(nblk, gpb * SSM_GROUP, gpb * p).astype(BF16)

    def out_map(c):
        t = c.reshape(nblk, gpb, SSM_GROUP, p)
        return jnp.einsum("ngcp,gh->ngphc", t, eye).reshape(nblk, gpb * p, gpb * SSM_GROUP).astype(BF16)

    return (lam_re.reshape(1, g * p), lam_im.reshape(1, g * p), in_map(bb_re), in_map(bb_im),
            out_map(c_re), out_map(c_im))


def _final_norm_kernel(x_ref, w_ref, o_ref):
    x = x_ref[...]
    ms = jnp.mean(x * x, axis=-1, keepdims=True)
    o_ref[...] = x * lax.rsqrt(ms + NORM_EPS) * w_ref[...]


def _final_norm(x3, w):
    l, b, d = x3.shape
    tl = min(ROW_TILE // b, l)
    x_spec, _ = _row_specs(tl, b, d)
    return pl.pallas_call(
        _final_norm_kernel,
        grid=(l // tl,),
        in_specs=[x_spec, pl.BlockSpec((1, d), lambda i: (0, 0))],
        out_specs=x_spec,
        out_shape=jax.ShapeDtypeStruct((l, b, d), F32),
        compiler_params=_params("arbitrary"),
        name="final_norm",
    )(x3, w)


def _trunk(xs, mods, s_hgrn, s_re, s_im, p):
    depth = p["norm_w"].shape[0]
    n_groups = len(xs)
    d = xs[0].shape[-1]
    n_hgrn = (depth + 1) // 2
    new_hgrn = [None] * n_groups
    new_re = [[] for _ in xs]
    new_im = [[] for _ in xs]
    for layer in range(depth):
        slot = layer // 2
        nw1 = p["norm_w"][layer, 0].reshape(1, d)
        nw2 = p["norm_w"][layer, 1].reshape(1, d)
        layer_mods = [m[layer] for m in mods]
        if layer % 2 == 0:
            for gi in range(n_groups):
                x3, mod = xs[gi], layer_mods[gi]
                q, lf, v, g = _hgrn_in(x3, mod, nw1, p["hgrn_w_in"], slot, p["log_lb"][slot], p["log1m_lb"][slot])
                o, new_hgrn[gi] = _gla(q, lf, v, g, p["hgrn_gnorm"][slot].reshape(1, HEAD_DIM),
                                       s_hgrn[gi], new_hgrn[gi], slot, n_hgrn)
                x3 = _proj_res(o, p["hgrn_w_out"], slot, x3, mod, 2)
                xs[gi] = _ffn(x3, mod, nw2, p["ffn_w_in"], p["ffn_w_out"], slot)
        else:
            lam_re, lam_im, bre, bim, cre, cim = p["s5"][slot]
            n_state = lam_re.shape[-1]
            for gi in range(n_groups):
                b = xs[gi].shape[1]
                if s_re[gi] is None:
                    x0r = x0i = jnp.zeros((b, n_state), F32)
                else:
                    x0r = s_re[gi][slot].reshape(b, n_state)
                    x0i = s_im[gi][slot].reshape(b, n_state)
                xs[gi], fr, fi = _s5(xs[gi], layer_mods[gi], nw1, x0r, x0i, lam_re, lam_im, bre, bim, cre, cim,
                                     p["ssm_d"][slot].reshape(1, d), p["ssm_w_glu"], slot)
                new_re[gi].append(fr)
                new_im[gi].append(fi)
            xs = _moe(xs, layer_mods, nw2, p["moe_w_router"][slot], p["moe_b_router"][slot],
                      p["moe_w_in"], p["moe_w_out"], slot)
    ys = [_final_norm(x3, p["final_norm_w"].reshape(1, d)) for x3 in xs]
    return ys, new_hgrn, [jnp.stack(r) for r in new_re], [jnp.stack(r) for r in new_im]


def kernel(x_prompt, x_sample, state_hgrn, state_ssm_re, state_ssm_im, c_prompt, c_sample, norm_w, ada_w, ada_b,
           hgrn_w_in, hgrn_lb, hgrn_gnorm, hgrn_w_out, ssm_a_re, ssm_a_im, ssm_log_dt, ssm_b_re, ssm_b_im,
           ssm_c_re, ssm_c_im, ssm_d, ssm_w_glu, ffn_w_in, ffn_w_out, moe_w_router, moe_b_router, moe_w_in,
           moe_w_out, final_norm_w):
    bp, lp, d = x_prompt.shape
    bs, ls, _ = x_sample.shape
    n_ssm, g, n_p = ssm_a_re.shape

    pr = jax.nn.softmax(hgrn_lb, axis=0)
    cs = jnp.cumsum(pr, axis=0)
    lb = cs - cs[0:1]
    p = {
        "norm_w": norm_w,
        "log_lb": jnp.log(lb)[:, None, :],
        "log1m_lb": jnp.log1p(-lb)[:, None, :],
        "hgrn_w_in": hgrn_w_in.astype(BF16),
        "hgrn_gnorm": hgrn_gnorm,
        "hgrn_w_out": hgrn_w_out.astype(BF16),
        "s5": [_s5_discretize(ssm_a_re[i], ssm_a_im[i], ssm_log_dt[i], ssm_b_re[i], ssm_b_im[i],
                              ssm_c_re[i], ssm_c_im[i]) for i in range(n_ssm)],
        "ssm_d": ssm_d,
        "ssm_w_glu": ssm_w_glu.astype(BF16),
        "ffn_w_in": ffn_w_in.astype(BF16),
        "ffn_w_out": ffn_w_out.astype(BF16),
        "moe_w_router": moe_w_router,
        "moe_b_router": moe_b_router,
        "moe_w_in": moe_w_in.astype(BF16),
        "moe_w_out": moe_w_out.astype(BF16),
        "final_norm_w": final_norm_w,
    }

    mods = _modulation(jnp.concatenate([c_prompt, c_sample], axis=0), ada_w, ada_b)
    mods_p = mods[:, :bp]
    mods_s = mods[:, bp:]

    xp = jnp.transpose(x_prompt, (1, 0, 2))
    xs = jnp.transpose(x_sample, (1, 0, 2))
    (yp, ys), (hgrn_p, hgrn_s), (re_p, re_s), (im_p, im_s) = _trunk(
        [xp, xs], [mods_p, mods_s], [None, state_hgrn], [None, state_ssm_re], [None, state_ssm_im], p)

    def ssm_shape(a, b):
        return a.reshape(n_ssm, b, g, n_p)

    return (jnp.transpose(yp, (1, 0, 2)), jnp.transpose(ys, (1, 0, 2)),
            hgrn_p, ssm_shape(re_p, bp), ssm_shape(im_p, bp),
            hgrn_s, ssm_shape(re_s, bs), ssm_shape(im_s, bs))
```

```python
import functools
import math

import jax
import jax.numpy as jnp
from jax import lax
from jax.experimental import pallas as pl
from jax.experimental.pallas import tpu as pltpu

F32 = jnp.float32
BF16 = jnp.bfloat16
HIGHEST = lax.Precision.HIGHEST

NORM_EPS = 1e-6
LOG2_E = 1.4426950408889634
LANES = 128
SUBLANES = 8
BF16_ROWS = 16
ROW_TILE = 512
HEAD_DIM = 128
GLA_CHUNK = 64
GLA_HEADS_PER_STEP = 32
SSM_GROUP = 16
SSM_LANE_CHUNK = 1024
MOE_TILE = 768
DMA_UNROLL = 8
VMEM_LIMIT = 56 * 1024 * 1024


def _params(*sem):
    return pltpu.CompilerParams(dimension_semantics=sem, vmem_limit_bytes=VMEM_LIMIT)


def _sigmoid(x):
    return 1.0 / (1.0 + jnp.exp(-x))


def _silu(x):
    return x * _sigmoid(x)


def _dot(a, b):
    return jnp.dot(a, b, preferred_element_type=F32)


def _tile_geometry(b, l):
    tl = min(l, ROW_TILE)
    bb = max(1, min(b, ROW_TILE // tl))
    assert l % tl == 0 and b % bb == 0
    return bb, tl


def _mod_rows(m_ref, b0, bb, tl):
    if bb == 1:
        return m_ref[pl.ds(b0, 1), :]
    rows = bb * tl
    n_b = m_ref.shape[0]
    r = lax.broadcasted_iota(jnp.int32, (rows, n_b), 0)
    c = lax.broadcasted_iota(jnp.int32, (rows, n_b), 1)
    expand = jnp.where(c == b0 + r // tl, 1.0, 0.0)
    return jnp.dot(expand, m_ref[...], preferred_element_type=F32, precision=HIGHEST)


def _norm_mod(x, nw, sc, sh):
    ms = jnp.mean(x * x, axis=-1, keepdims=True)
    return x * lax.rsqrt(ms + NORM_EPS) * nw * (1.0 + sc) + sh


def _row_specs(rows, d, n_b):
    x_spec = pl.BlockSpec((rows, d), lambda i, *_: (i, 0))

    def mod_spec(k):
        return pl.BlockSpec((n_b, d), lambda i, *_: (0, k))

    return x_spec, mod_spec


def _mod_kernel(c_ref, w_ref, b_ref, o_ref):
    c = c_ref[...]
    o_ref[0] = _dot(_silu(c).astype(BF16), w_ref[0].astype(BF16)) + b_ref[0]


def _modulation(c_all, ada_w, ada_b):
    depth, d, n = ada_w.shape
    rows = c_all.shape[0]
    tn = 1536
    assert n % tn == 0
    return pl.pallas_call(
        _mod_kernel,
        grid=(depth, n // tn),
        in_specs=[
            pl.BlockSpec((rows, d), lambda l, j: (0, 0)),
            pl.BlockSpec((1, d, tn), lambda l, j: (l, 0, j)),
            pl.BlockSpec((1, 1, tn), lambda l, j: (l, 0, j)),
        ],
        out_specs=pl.BlockSpec((1, rows, tn), lambda l, j: (l, 0, j)),
        out_shape=jax.ShapeDtypeStruct((depth, rows, n), F32),
        compiler_params=_params("arbitrary", "arbitrary"),
        name="adaln_modulation",
    )(c_all, ada_w, ada_b.reshape(depth, 1, n))


def _hgrn_in_kernel(x_ref, nw_ref, sh_ref, sc_ref, w_ref, la_ref, lc_ref, q_ref, lf_ref, v_ref, g_ref,
                    *, bb, tl, l_total):
    rows, d = x_ref.shape
    b0 = (pl.program_id(0) * rows) // l_total
    h = _norm_mod(x_ref[...], nw_ref[...], _mod_rows(sc_ref, b0, bb, tl), _mod_rows(sh_ref, b0, bb, tl))
    h = h.astype(BF16)
    q_ref[...] = _silu(_dot(h, w_ref[:, 0:d])).astype(q_ref.dtype)
    f = _dot(h, w_ref[:, d : 2 * d])
    log_sig = jnp.minimum(f, 0.0) - jnp.log1p(jnp.exp(-jnp.abs(f)))
    c = lc_ref[...] + log_sig
    a = la_ref[...]
    lf_ref[...] = jnp.maximum(a, c) + jnp.log1p(jnp.exp(-jnp.abs(a - c)))
    v_ref[...] = _dot(h, w_ref[:, 2 * d : 3 * d]).astype(v_ref.dtype)
    g_ref[...] = _silu(_dot(h, w_ref[:, 3 * d : 4 * d])).astype(g_ref.dtype)


def _hgrn_in(x2, b, l, mod, nw, w_in, slot, log_lb, log1m_lb):
    t, d = x2.shape
    bb, tl = _tile_geometry(b, l)
    rows = bb * tl
    x_spec, mod_spec = _row_specs(rows, d, b)
    row = pl.BlockSpec((1, d), lambda i: (0, 0))
    act = BF16 if l % BF16_ROWS == 0 else F32
    return pl.pallas_call(
        functools.partial(_hgrn_in_kernel, bb=bb, tl=tl, l_total=l),
        grid=(t // rows,),
        in_specs=[x_spec, row, mod_spec(0), mod_spec(1),
                  pl.BlockSpec((None, d, 4 * d), lambda i: (slot, 0, 0)), row, row],
        out_specs=[x_spec, x_spec, x_spec, x_spec],
        out_shape=[jax.ShapeDtypeStruct((t, d), act), jax.ShapeDtypeStruct((t, d), F32),
                   jax.ShapeDtypeStruct((t, d), act), jax.ShapeDtypeStruct((t, d), act)],
        compiler_params=_params("arbitrary"),
        name="hgrn_in_proj",
    )(x2, nw, mod, mod, w_in, log_lb, log1m_lb)


def _cumsum_rows(x):
    n = x.shape[0]
    row = lax.broadcasted_iota(jnp.int32, x.shape, 0)
    shift = 1
    while shift < n:
        x = x + jnp.where(row >= shift, pltpu.roll(x, shift, axis=0), 0.0)
        shift *= 2
    return x


def _gla_head(q, lf, v, g, gw, s, c):
    nb = c // SUBLANES
    lf2 = lf * LOG2_E
    b = _cumsum_rows(lf2)
    k = 1.0 - jnp.exp2(lf2)
    vb = v.astype(BF16)

    o = _dot((q * jnp.exp2(b)).astype(BF16), s.astype(BF16))

    b3 = b.reshape(nb, SUBLANES, HEAD_DIM)
    q3 = q.reshape(nb, SUBLANES, HEAD_DIM)
    k3 = k.reshape(nb, SUBLANES, HEAD_DIM)
    v3 = v.reshape(nb, SUBLANES, HEAD_DIM)
    t_in = lax.broadcasted_iota(jnp.int32, (nb, SUBLANES, HEAD_DIM), 1)
    o_diag = jnp.zeros((nb, SUBLANES, HEAD_DIM), F32)
    for s_in in range(SUBLANES):
        diff = b3 - b3[:, s_in : s_in + 1, :]
        e = jnp.exp2(jnp.where(t_in >= s_in, diff, -jnp.inf))
        w = jnp.sum(q3 * e * k3[:, s_in : s_in + 1, :], axis=-1, keepdims=True)
        o_diag = o_diag + w * v3[:, s_in : s_in + 1, :]
    o = o + o_diag.reshape(c, HEAD_DIM)

    if c > SUBLANES:
        row = lax.broadcasted_iota(jnp.int32, (c, c), 0)
        col = lax.broadcasted_iota(jnp.int32, (c, c), 1)
        t_row = lax.broadcasted_iota(jnp.int32, (c, HEAD_DIM), 0)
        a = jnp.zeros((c, c), F32)
        half = SUBLANES
        while half < c:
            span = 2 * half
            m = b.reshape(c // span, span, HEAD_DIM)[:, half - 1 : half, :]
            m = jnp.broadcast_to(m, (c // span, span, HEAD_DIM)).reshape(c, HEAD_DIM)
            right = (t_row & half) != 0
            qh = q * jnp.exp2(jnp.where(right, b - m, -jnp.inf))
            kh = k * jnp.exp2(jnp.where(right, -jnp.inf, m - b))
            sc = lax.dot_general(qh.astype(BF16), kh.astype(BF16), (((1,), (1,)), ((), ())),
                                 preferred_element_type=F32)
            shift = int(math.log2(span))
            a = a + jnp.where((row >> shift) == (col >> shift), sc, 0.0)
            half = span
        o = o + _dot(a.astype(BF16), vb)

    b_last = b[c - 1 : c, :]
    kd = k * jnp.exp2(b_last - b)
    pad = HEAD_DIM - c
    x = jnp.concatenate([kd, jnp.broadcast_to(jnp.exp2(b_last), (pad, HEAD_DIM))], axis=0)
    xt = x.T
    v_pad = jnp.concatenate([v, jnp.zeros((pad, HEAD_DIM), F32)], axis=0).astype(BF16)
    s_new = xt[:, c : c + 1] * s + _dot(xt.astype(BF16), v_pad)

    ms = jnp.mean(o * o, axis=-1, keepdims=True)
    return o * lax.rsqrt(ms + NORM_EPS) * gw * g, s_new


def _gla_kernel(*refs, c, heads, n_chunks, has_s0):
    q_ref, lf_ref, v_ref, g_ref, gw_ref = refs[:5]
    s0_ref = refs[5] if has_s0 else None
    o_ref, sf_ref, s_scr = refs[-3:]
    step = pl.program_id(1)

    @pl.when(step == 0)
    def _():
        if has_s0:
            s_scr[...] = s0_ref[...]
        else:
            s_scr[...] = jnp.zeros_like(s_scr)

    gw = gw_ref[...]
    for bi in range(q_ref.shape[0]):
        for h in range(heads):
            sl = slice(h * HEAD_DIM, (h + 1) * HEAD_DIM)
            out, s_new = _gla_head(q_ref[bi, :, sl].astype(F32), lf_ref[bi, :, sl], v_ref[bi, :, sl].astype(F32),
                                   g_ref[bi, :, sl].astype(F32), gw, s_scr[bi, h], c)
            o_ref[bi, :, sl] = out.astype(o_ref.dtype)
            s_scr[bi, h] = s_new

    @pl.when(step == n_chunks - 1)
    def _():
        sf_ref[...] = s_scr[...]


def _gla(q2, lf2, v2, g2, b, l, gnorm_w, s0_all, s_prev, slot, n_slots):
    d = q2.shape[-1]
    heads = d // HEAD_DIM
    c = math.gcd(l, GLA_CHUNK)
    n_chunks = l // c
    bt = max(1, min(b, GLA_HEADS_PER_STEP // heads // max(1, c // 32)))
    assert b % bt == 0
    has_s0 = s0_all is not None
    blk = pl.BlockSpec((bt, c, d), lambda i, j: (i, j, 0))
    s_spec = pl.BlockSpec((None, bt, heads, HEAD_DIM, HEAD_DIM), lambda i, j: (slot, i, 0, 0, 0))
    args = [a.reshape(b, l, d) for a in (q2, lf2, v2, g2)] + [gnorm_w]
    in_specs = [blk, blk, blk, blk, pl.BlockSpec((1, HEAD_DIM), lambda i, j: (0, 0))]
    if has_s0:
        args.append(s0_all)
        in_specs.append(s_spec)
    if s_prev is not None:
        args.append(s_prev)
        in_specs.append(pl.BlockSpec(memory_space=pl.ANY))
    o, s_fin = pl.pallas_call(
        functools.partial(_gla_kernel, c=c, heads=heads, n_chunks=n_chunks, has_s0=has_s0),
        grid=(b // bt, n_chunks),
        in_specs=in_specs,
        out_specs=[blk, s_spec],
        out_shape=[jax.ShapeDtypeStruct((b, l, d), q2.dtype),
                   jax.ShapeDtypeStruct((n_slots, b, heads, HEAD_DIM, HEAD_DIM), F32)],
        input_output_aliases={len(args) - 1: 1} if s_prev is not None else {},
        scratch_shapes=[pltpu.VMEM((bt, heads, HEAD_DIM, HEAD_DIM), F32)],
        compiler_params=_params("arbitrary", "arbitrary"),
        name="hgrn_gla_scan",
    )(*args)
    return o.reshape(b * l, d), s_fin


def _proj_res_kernel(a_ref, w_ref, x_ref, gt_ref, o_ref, *, bb, tl, l_total):
    rows = x_ref.shape[0]
    b0 = (pl.program_id(0) * rows) // l_total
    y = _dot(a_ref[...].astype(BF16), w_ref[...])
    o_ref[...] = x_ref[...] + _mod_rows(gt_ref, b0, bb, tl) * y


def _proj_res(a2, w, slot, x2, b, l, mod, gate_idx):
    t, d = x2.shape
    k = a2.shape[-1]
    bb, tl = _tile_geometry(b, l)
    rows = bb * tl
    x_spec, mod_spec = _row_specs(rows, d, b)
    return pl.pallas_call(
        functools.partial(_proj_res_kernel, bb=bb, tl=tl, l_total=l),
        grid=(t // rows,),
        in_specs=[pl.BlockSpec((rows, k), lambda i: (i, 0)), pl.BlockSpec((None, k, d), lambda i: (slot, 0, 0)),
                  x_spec, mod_spec(gate_idx)],
        out_specs=x_spec,
        out_shape=jax.ShapeDtypeStruct((t, d), F32),
        compiler_params=_params("arbitrary"),
        name="proj_residual",
    )(a2, w, x2, mod)


def _ff_tile(d_ff):
    best = LANES
    for t in range(LANES, 1408 + 1, LANES):
        if d_ff % t == 0:
            best = t
    return best


def _ffn_kernel(x_ref, nw_ref, sh_ref, sc_ref, gt_ref, wg_ref, wu_ref, wo_ref, o_ref, h_scr, acc_scr,
                *, nf, bb, tl, l_total):
    rows = x_ref.shape[0]
    b0 = (pl.program_id(0) * rows) // l_total
    j = pl.program_id(1)

    @pl.when(j == 0)
    def _():
        h = _norm_mod(x_ref[...], nw_ref[...], _mod_rows(sc_ref, b0, bb, tl), _mod_rows(sh_ref, b0, bb, tl))
        h_scr[...] = h.astype(BF16)
        acc_scr[...] = jnp.zeros_like(acc_scr)

    h = h_scr[...]
    act = (_silu(_dot(h, wg_ref[...])) * _dot(h, wu_ref[...])).astype(BF16)
    acc_scr[...] += _dot(act, wo_ref[...])

    @pl.when(j == nf - 1)
    def _():
        o_ref[...] = x_ref[...] + _mod_rows(gt_ref, b0, bb, tl) * acc_scr[...]


def _ffn(x2, b, l, mod, nw, w_in, w_out, slot):
    t, d = x2.shape
    d_ff = w_out.shape[-2]
    tf = _ff_tile(d_ff)
    nf = d_ff // tf
    bb, tl = _tile_geometry(b, l)
    rows = bb * tl
    x_spec, mod_spec = _row_specs(rows, d, b)
    return pl.pallas_call(
        functools.partial(_ffn_kernel, nf=nf, bb=bb, tl=tl, l_total=l),
        grid=(t // rows, nf),
        in_specs=[
            x_spec, pl.BlockSpec((1, d), lambda i, j: (0, 0)), mod_spec(3), mod_spec(4), mod_spec(5),
            pl.BlockSpec((None, d, tf), lambda i, j: (slot, 0, j)),
            pl.BlockSpec((None, d, tf), lambda i, j: (slot, 0, nf + j)),
            pl.BlockSpec((None, tf, d), lambda i, j: (slot, j, 0)),
        ],
        out_specs=x_spec,
        out_shape=jax.ShapeDtypeStruct((t, d), F32),
        scratch_shapes=[pltpu.VMEM((rows, d), BF16), pltpu.VMEM((rows, d), F32)],
        compiler_params=_params("arbitrary", "arbitrary"),
        name="dense_swiglu",
    )(x2, nw, mod, mod, mod, w_in, w_in, w_out)


def _route_kernel(x_ref, nw_ref, sh_ref, sc_ref, wr_ref, br_ref, cnt0_ref, h_ref, pos_ref, prob_ref, cnt_ref,
                  cnt_scr, *, cap, n_steps, bb, tl, l_total):
    rows, d = x_ref.shape
    step = pl.program_id(0)
    b0 = (step * rows) // l_total

    @pl.when(step == 0)
    def _():
        cnt_scr[...] = cnt0_ref[...]

    h = _norm_mod(x_ref[...], nw_ref[...], _mod_rows(sc_ref, b0, bb, tl), _mod_rows(sh_ref, b0, bb, tl))
    h_ref[...] = h
    logits = jnp.dot(h, wr_ref[...], preferred_element_type=F32, precision=HIGHEST) + br_ref[...]
    lane = lax.broadcasted_iota(jnp.int32, (rows, LANES), 1)
    m0 = jnp.max(logits, axis=-1, keepdims=True)
    i0 = jnp.min(jnp.where(logits == m0, lane, LANES), axis=-1, keepdims=True)
    rest = jnp.where(lane == i0, -jnp.inf, logits)
    m1 = jnp.max(rest, axis=-1, keepdims=True)
    i1 = jnp.min(jnp.where(rest == m1, lane, LANES), axis=-1, keepdims=True)
    e1 = jnp.exp(m1 - m0)
    p0 = 1.0 / (1.0 + e1)
    p1 = e1 / (1.0 + e1)
    hit0 = lane == i0
    hit1 = lane == i1
    onehot = jnp.where(hit0 | hit1, 1.0, 0.0)
    r_i = lax.broadcasted_iota(jnp.int32, (rows, rows), 0)
    c_i = lax.broadcasted_iota(jnp.int32, (rows, rows), 1)
    before = _dot((r_i > c_i).astype(BF16), onehot.astype(BF16)) + cnt_scr[...]
    rank0 = jnp.sum(jnp.where(hit0, before, 0.0), axis=-1, keepdims=True).astype(jnp.int32)
    rank1 = jnp.sum(jnp.where(hit1, before, 0.0), axis=-1, keepdims=True).astype(jnp.int32)
    pos_ref[...] = jnp.where(lane == 0, i0 * cap + rank0, jnp.where(lane == 1, i1 * cap + rank1, 0))
    prob_ref[...] = jnp.where(lane == 0, p0, jnp.where(lane == 1, p1, 0.0))
    cnt_scr[...] += jnp.sum(onehot, axis=0, keepdims=True)

    @pl.when(step == n_steps - 1)
    def _():
        cnt_ref[...] = cnt_scr[...]


def _route(x2, b, l, mod, nw, wr, br, cnt0, cap):
    t, d = x2.shape
    bb, tl = _tile_geometry(b, l)
    rows = bb * tl
    n_steps = t // rows
    x_spec, mod_spec = _row_specs(rows, d, b)
    row = pl.BlockSpec((1, LANES), lambda i: (0, 0))
    lanes_spec = pl.BlockSpec((rows, LANES), lambda i: (i, 0))
    return pl.pallas_call(
        functools.partial(_route_kernel, cap=cap, n_steps=n_steps, bb=bb, tl=tl, l_total=l),
        grid=(n_steps,),
        in_specs=[x_spec, pl.BlockSpec((1, d), lambda i: (0, 0)), mod_spec(3), mod_spec(4),
                  pl.BlockSpec((d, LANES), lambda i: (0, 0)), row, row],
        out_specs=[x_spec, lanes_spec, lanes_spec, row],
        out_shape=[jax.ShapeDtypeStruct((t, d), F32), jax.ShapeDtypeStruct((t, LANES), jnp.int32),
                   jax.ShapeDtypeStruct((t, LANES), F32), jax.ShapeDtypeStruct((1, LANES), F32)],
        scratch_shapes=[pltpu.VMEM((1, LANES), F32)],
        compiler_params=_params("arbitrary"),
        name="moe_route",
    )(x2, nw, mod, mod, wr, br, cnt0)


def _row_copy(src, src_row, dst, dst_row, sem):
    return pltpu.make_async_copy(src.at[pl.ds(src_row, 1)], dst.at[pl.ds(dst_row, 1)], sem)


def _scatter_kernel(cnt_ref, pos_ref, h_ref, *rest, rows, cap, n_steps, fill):
    if fill:
        _, xg_ref, zero_row, sem = rest
    else:
        xg_ref, zero_row, sem = rest

    def issue(it, carry):
        for u in range(DMA_UNROLL):
            r = it * DMA_UNROLL + u
            _row_copy(h_ref, r, xg_ref, pos_ref[0, 0, 2 * r], sem).start(priority=0)
            _row_copy(h_ref, r, xg_ref, pos_ref[0, 0, 2 * r + 1], sem).start(priority=1)
        return carry

    lax.fori_loop(0, rows // DMA_UNROLL, issue, 0)
    for _ in range(2):
        pltpu.make_async_copy(h_ref, xg_ref.at[pl.ds(0, rows)], sem).wait()

    if fill:
        @pl.when(pl.program_id(0) == n_steps - 1)
        def _():
            zero_row[...] = jnp.zeros_like(zero_row)
            for e in range(cnt_ref.shape[0]):
                cnt = cnt_ref[e]
                n_pad = lax.rem(MOE_TILE - lax.rem(cnt, MOE_TILE), MOE_TILE)

                def put(r, carry, base=e * cap + cnt):
                    _row_copy(zero_row, 0, xg_ref, base + r, sem).start()
                    return carry

                def done(r, carry):
                    _row_copy(zero_row, 0, xg_ref, 0, sem).wait()
                    return carry

                lax.fori_loop(0, n_pad, put, 0)
                lax.fori_loop(0, n_pad, done, 0)


def _scatter(counts, pos, h2, xg, n_slots, cap):
    t, d = h2.shape
    rows = min(ROW_TILE, t)
    assert rows % DMA_UNROLL == 0
    n_steps = t // rows
    fill = xg is not None
    in_specs = [pl.BlockSpec((1, 1, 2 * rows), lambda i, c: (i, 0, 0), memory_space=pltpu.SMEM),
                pl.BlockSpec((rows, d), lambda i, c: (i, 0))]
    args = [counts, pos, h2]
    if fill:
        in_specs.append(pl.BlockSpec(memory_space=pl.ANY))
        args.append(xg)
    return pl.pallas_call(
        functools.partial(_scatter_kernel, rows=rows, cap=cap, n_steps=n_steps, fill=fill),
        grid_spec=pltpu.PrefetchScalarGridSpec(
            num_scalar_prefetch=1, grid=(n_steps,), in_specs=in_specs,
            out_specs=pl.BlockSpec(memory_space=pl.ANY),
            scratch_shapes=[pltpu.VMEM((1, d), F32), pltpu.SemaphoreType.DMA(())]),
        out_shape=jax.ShapeDtypeStruct((n_slots, d), F32),
        input_output_aliases={3: 0} if fill else {},
        compiler_params=_params("arbitrary"),
        name="moe_scatter",
    )(*args)


def _gmm_kernel(blk_ref, exp_ref, used_ref, x_ref, wg_ref, wu_ref, wo_ref, o_ref, xb_scr):
    i = pl.program_id(0)
    j = pl.program_id(1)

    @pl.when(i < used_ref[0])
    def _():
        @pl.when(j == 0)
        def _():
            xb_scr[...] = x_ref[...].astype(BF16)

        h = xb_scr[...]
        act = (_silu(_dot(h, wg_ref[...])) * _dot(h, wu_ref[...])).astype(BF16)
        y = _dot(act, wo_ref[...])

        @pl.when(j == 0)
        def _():
            o_ref[...] = y

        @pl.when(j > 0)
        def _():
            o_ref[...] += y


def _gmm(tile_blk, tile_exp, n_used, xg, w_in, w_out, slot):
    n_slots, d = xg.shape
    d_ff = w_out.shape[-2]
    tf = _ff_tile(d_ff)
    nf = d_ff // tf
    n_tiles = tile_blk.shape[0]

    def ff(i, j, used):
        return jnp.where(i < used[0], j, nf - 1)

    row_spec = pl.BlockSpec((MOE_TILE, d), lambda i, j, blk, ex, used: (blk[i], 0))
    return pl.pallas_call(
        _gmm_kernel,
        grid_spec=pltpu.PrefetchScalarGridSpec(
            num_scalar_prefetch=3, grid=(n_tiles, nf),
            in_specs=[
                row_spec,
                pl.BlockSpec((None, None, d, tf), lambda i, j, blk, ex, used: (slot, ex[i], 0, ff(i, j, used))),
                pl.BlockSpec((None, None, d, tf),
                             lambda i, j, blk, ex, used: (slot, ex[i], 0, nf + ff(i, j, used))),
                pl.BlockSpec((None, None, tf, d), lambda i, j, blk, ex, used: (slot, ex[i], ff(i, j, used), 0)),
            ],
            out_specs=row_spec,
            scratch_shapes=[pltpu.VMEM((MOE_TILE, d), BF16)]),
        out_shape=jax.ShapeDtypeStruct((n_slots, d), F32),
        compiler_params=_params("arbitrary", "arbitrary"),
        name="moe_grouped_swiglu",
    )(tile_blk, tile_exp, n_used, xg, w_in, w_in, w_out)


def _combine_kernel(pos_ref, x_ref, gt_ref, prob_ref, y_ref, o_ref, buf0, buf1, sem, *, bb, tl, l_total):
    rows = x_ref.shape[0]
    b0 = (pl.program_id(0) * rows) // l_total

    def issue(it, carry):
        for u in range(DMA_UNROLL):
            r = it * DMA_UNROLL + u
            _row_copy(y_ref, pos_ref[0, 0, 2 * r], buf0, r, sem).start(priority=0)
            _row_copy(y_ref, pos_ref[0, 0, 2 * r + 1], buf1, r, sem).start(priority=1)
        return carry

    lax.fori_loop(0, rows // DMA_UNROLL, issue, 0)
    for buf in (buf0, buf1):
        pltpu.make_async_copy(y_ref.at[pl.ds(0, rows)], buf, sem).wait()
    prob = prob_ref[...]
    f = prob[:, 0:1] * buf0[...] + prob[:, 1:2] * buf1[...]
    o_ref[...] = x_ref[...] + _mod_rows(gt_ref, b0, bb, tl) * f


def _combine(pos, x2, b, l, mod, prob, y):
    t, d = x2.shape
    bb, tl = _tile_geometry(b, l)
    rows = bb * tl
    assert rows % DMA_UNROLL == 0
    x_spec, mod_spec = _row_specs(rows, d, b)
    return pl.pallas_call(
        functools.partial(_combine_kernel, bb=bb, tl=tl, l_total=l),
        grid=(t // rows,),
        in_specs=[pl.BlockSpec((1, 1, 2 * rows), lambda i: (i, 0, 0), memory_space=pltpu.SMEM),
                  x_spec, mod_spec(5), pl.BlockSpec((rows, LANES), lambda i: (i, 0)),
                  pl.BlockSpec(memory_space=pl.ANY)],
        out_specs=x_spec,
        out_shape=jax.ShapeDtypeStruct((t, d), F32),
        scratch_shapes=[pltpu.VMEM((rows, d), F32), pltpu.VMEM((rows, d), F32), pltpu.SemaphoreType.DMA(())],
        compiler_params=_params("arbitrary"),
        name="moe_combine",
    )(pos, x2, mod, prob, y)


def _moe(xs, dims, mods, nw, w_router, b_router, w_in, w_out, slot):
    d = xs[0].shape[-1]
    n_exp = w_in.shape[1]
    t_total = sum(x.shape[0] for x in xs)
    cap = -(-t_total // MOE_TILE) * MOE_TILE
    n_slots = n_exp * cap
    n_tiles = -(-(2 * t_total + n_exp * (MOE_TILE - 1)) // MOE_TILE)
    wr = jnp.zeros((d, LANES), F32).at[:, :n_exp].set(w_router)
    br = jnp.full((1, LANES), -jnp.inf, F32).at[0, :n_exp].set(b_router)

    cnt = jnp.zeros((1, LANES), F32)
    routed = []
    for x2, (b, l), mod in zip(xs, dims, mods):
        h2, pos, prob, cnt = _route(x2, b, l, mod, nw, wr, br, cnt, cap)
        rows = min(ROW_TILE, pos.shape[0])
        routed.append((h2, pos[:, :2].reshape(pos.shape[0] // rows, 1, 2 * rows), prob))
    counts = cnt[0, :n_exp].astype(jnp.int32)

    xg = None
    for h2, pos, _ in routed:
        xg = _scatter(counts, pos, h2, xg, n_slots, cap)

    tiles = (counts + (MOE_TILE - 1)) // MOE_TILE
    ends = jnp.cumsum(tiles)
    n_used = ends[-1:]
    idx = jnp.minimum(jnp.arange(n_tiles, dtype=jnp.int32), n_used - 1)
    tile_exp = jnp.sum((idx[:, None] >= ends[None, :]).astype(jnp.int32), axis=1)
    tile_blk = tile_exp * (cap // MOE_TILE) + idx - (ends - tiles)[tile_exp]
    y = _gmm(tile_blk.astype(jnp.int32), tile_exp, n_used.astype(jnp.int32), xg, w_in, w_out, slot)

    return [_combine(pos, x2, b, l, mod, prob, y)
            for x2, (b, l), mod, (_, pos, prob) in zip(xs, dims, mods, routed)]


def _gelu_tanh(x):
    return 0.5 * x * (1.0 + jnp.tanh(math.sqrt(2.0 / math.pi) * (x + 0.044715 * (x * x * x))))


def _s5_kernel(x_ref, nw_ref, sh_ref, sc_ref, gt_ref, bre_ref, bim_ref, cre_ref, cim_ref, lr_ref, li_ref,
               dsk_ref, wglu_ref, x0r_ref, x0i_ref, o_ref, fr_ref, fi_ref,
               u_scr, bur, bui, y_scr, sr, si, *, n_steps):
    bb, tc, d = x_ref.shape
    rows = bb * tc
    n_lane_chunks = bre_ref.shape[0]
    in_chunk = bre_ref.shape[1]
    half_lanes = SSM_LANE_CHUNK // 2
    b0 = pl.program_id(0) * bb
    t = pl.program_id(1)

    @pl.when(t == 0)
    def _():
        sr[...] = x0r_ref[...]
        si[...] = x0i_ref[...]

    x = x_ref[...].reshape(rows, d)
    h = _norm_mod(x, nw_ref[...], _mod_rows(sc_ref, b0, bb, tc), _mod_rows(sh_ref, b0, bb, tc))

    r_i = lax.broadcasted_iota(jnp.int32, (rows, rows), 0)
    c_i = lax.broadcasted_iota(jnp.int32, (rows, rows), 1)
    to_time_major = (c_i == (r_i % bb) * tc + r_i // bb).astype(BF16)
    to_batch_major = (c_i == (r_i % tc) * bb + r_i // tc).astype(BF16)
    h_hi = h.astype(BF16)
    h_lo = (h - h_hi.astype(F32)).astype(BF16)
    u_scr[...] = _dot(to_time_major, h_hi) + _dot(to_time_major, h_lo)

    for n in range(n_lane_chunks):
        u = u_scr[:, n * in_chunk : (n + 1) * in_chunk].astype(BF16)
        bur[...] = _dot(u, bre_ref[n])
        bui[...] = _dot(u, bim_ref[n])
        for hf in range(2):
            ls = slice(hf * half_lanes, (hf + 1) * half_lanes)
            g0 = n * SSM_LANE_CHUNK + hf * half_lanes
            lam_r = jnp.broadcast_to(lr_ref[:, g0 : g0 + half_lanes], (SUBLANES, half_lanes))
            lam_i = jnp.broadcast_to(li_ref[:, g0 : g0 + half_lanes], (SUBLANES, half_lanes))

            def batch_tile(bt, _):
                r0 = pl.multiple_of(bt * SUBLANES, SUBLANES)

                def step(l, carry):
                    xr, xi = carry
                    row = pl.multiple_of(l * bb + r0, SUBLANES)
                    nr = lam_r * xr - lam_i * xi + bur[pl.ds(row, SUBLANES), ls]
                    ni = lam_r * xi + lam_i * xr + bui[pl.ds(row, SUBLANES), ls]
                    bur[pl.ds(row, SUBLANES), ls] = nr
                    bui[pl.ds(row, SUBLANES), ls] = ni
                    return nr, ni

                xr, xi = lax.fori_loop(
                    0, tc, step,
                    (sr[pl.ds(r0, SUBLANES), g0 : g0 + half_lanes], si[pl.ds(r0, SUBLANES), g0 : g0 + half_lanes]),
                    unroll=8)
                sr[pl.ds(r0, SUBLANES), g0 : g0 + half_lanes] = xr
                si[pl.ds(r0, SUBLANES), g0 : g0 + half_lanes] = xi
                return 0

            lax.fori_loop(0, bb // SUBLANES, batch_tile, 0)
        y_scr[:, n * in_chunk : (n + 1) * in_chunk] = (
            _dot(bur[...].astype(BF16), cre_ref[n]) - _dot(bui[...].astype(BF16), cim_ref[n]))

    y = y_scr[...] + dsk_ref[...] * u_scr[...]
    z = _dot(to_batch_major, _gelu_tanh(y).astype(BF16)).astype(BF16)
    zz = _dot(z, wglu_ref[...])
    m = zz[:, :d] * _sigmoid(zz[:, d:])
    o_ref[...] = (x + _mod_rows(gt_ref, b0, bb, tc) * m).reshape(bb, tc, d)

    @pl.when(t == n_steps - 1)
    def _():
        fr_ref[...] = sr[...]
        fi_ref[...] = si[...]


def _s5(x2, b, l, mod, nw, x0_re, x0_im, lam_re, lam_im, b_re_blk, b_im_blk, c_re_blk, c_im_blk, d_skip, w_glu,
        slot):
    t, d = x2.shape
    n_state = lam_re.shape[-1]
    if b * l <= ROW_TILE:
        bb, tc = b, l
    elif b <= ROW_TILE // SUBLANES:
        bb, tc = b, ROW_TILE // b
    else:
        tc = l
        bb = ROW_TILE // l
    assert l % tc == 0 and b % bb == 0 and bb % SUBLANES == 0 and tc % SUBLANES == 0
    n_steps = l // tc
    rows = tc * bb
    x_spec = pl.BlockSpec((bb, tc, d), lambda i, s: (i, s, 0))

    def mod_spec(k):
        return pl.BlockSpec((b, d), lambda i, s: (0, k))

    def full(a):
        nd = a.ndim
        return pl.BlockSpec(a.shape, lambda i, s: (0,) * nd)

    st_spec = pl.BlockSpec((bb, n_state), lambda i, s: (i, 0))
    out, fr, fi = pl.pallas_call(
        functools.partial(_s5_kernel, n_steps=n_steps),
        grid=(b // bb, n_steps),
        in_specs=[x_spec, full(nw), mod_spec(0), mod_spec(1), mod_spec(2), full(b_re_blk), full(b_im_blk),
                  full(c_re_blk), full(c_im_blk), full(lam_re), full(lam_im), full(d_skip),
                  pl.BlockSpec((None,) + w_glu.shape[1:], lambda i, s: (slot, 0, 0)),
                  st_spec, st_spec],
        out_specs=[x_spec, st_spec, st_spec],
        out_shape=[jax.ShapeDtypeStruct((b, l, d), F32), jax.ShapeDtypeStruct((b, n_state), F32),
                   jax.ShapeDtypeStruct((b, n_state), F32)],
        scratch_shapes=[pltpu.VMEM((rows, d), F32), pltpu.VMEM((rows, SSM_LANE_CHUNK), F32),
                        pltpu.VMEM((rows, SSM_LANE_CHUNK), F32), pltpu.VMEM((rows, d), F32),
                        pltpu.VMEM((bb, n_state), F32), pltpu.VMEM((bb, n_state), F32)],
        compiler_params=_params("arbitrary", "arbitrary"),
        name="s5_mixer",
    )(x2.reshape(b, l, d), nw, mod, mod, mod, b_re_blk, b_im_blk, c_re_blk, c_im_blk, lam_re, lam_im, d_skip,
      w_glu, x0_re, x0_im)
    return out.reshape(t, d), fr, fi


def _s5_discretize(a_re, a_im, log_dt, b_re, b_im, c_re, c_im):
    g, p = a_re.shape
    dt = jnp.exp(log_dt)[:, None]
    mag = jnp.exp(a_re * dt)
    lam_re = mag * jnp.cos(a_im * dt)
    lam_im = mag * jnp.sin(a_im * dt)
    den = a_re * a_re + a_im * a_im
    coef_re = ((lam_re - 1.0) * a_re + lam_im * a_im) / den
    coef_im = (lam_im * a_re - (lam_re - 1.0) * a_im) / den
    bb_re = coef_re[..., None] * b_re - coef_im[..., None] * b_im
    bb_im = coef_re[..., None] * b_im + coef_im[..., None] * b_re
    gpb = SSM_LANE_CHUNK // p
    nblk = g // gpb
    eye = jnp.eye(gpb, dtype=F32)

    def in_map(bb):
        t = bb.reshape(nblk, gpb, p, SSM_GROUP)
        return jnp.einsum("ngpc,gh->ngchp", t, eye).reshape(nblk, gpb * SSM_GROUP, gpb * p).astype(BF16)

    def out_map(c):
        t = c.reshape(nblk, gpb, SSM_GROUP, p)
        return jnp.einsum("ngcp,gh->ngphc", t, eye).reshape(nblk, gpb * p, gpb * SSM_GROUP).astype(BF16)

    return (lam_re.reshape(1, g * p), lam_im.reshape(1, g * p), in_map(bb_re), in_map(bb_im),
            out_map(c_re), out_map(c_im))


def _final_norm_kernel(x_ref, w_ref, o_ref):
    x = x_ref[...]
    ms = jnp.mean(x * x, axis=-1, keepdims=True)
    o_ref[...] = x * lax.rsqrt(ms + NORM_EPS) * w_ref[...]


def _final_norm(x2, w):
    t, d = x2.shape
    rows = min(ROW_TILE, t)
    x_spec = pl.BlockSpec((rows, d), lambda i: (i, 0))
    return pl.pallas_call(
        _final_norm_kernel,
        grid=(t // rows,),
        in_specs=[x_spec, pl.BlockSpec((1, d), lambda i: (0, 0))],
        out_specs=x_spec,
        out_shape=jax.ShapeDtypeStruct((t, d), F32),
        compiler_params=_params("arbitrary"),
        name="final_norm",
    )(x2, w)


def _trunk(xs, dims, mods, s_hgrn, s_re, s_im, p):
    depth = p["norm_w"].shape[0]
    n_groups = len(xs)
    d = xs[0].shape[-1]
    n_hgrn = (depth + 1) // 2
    new_hgrn = [None] * n_groups
    new_re = [[] for _ in xs]
    new_im = [[] for _ in xs]
    for layer in range(depth):
        slot = layer // 2
        nw1 = p["norm_w"][layer, 0].reshape(1, d)
        nw2 = p["norm_w"][layer, 1].reshape(1, d)
        layer_mods = [m[layer] for m in mods]
        if layer % 2 == 0:
            for gi in range(n_groups):
                x2, (b, l), mod = xs[gi], dims[gi], layer_mods[gi]
                q, lf, v, g = _hgrn_in(x2, b, l, mod, nw1, p["hgrn_w_in"], slot, p["log_lb"][slot],
                                       p["log1m_lb"][slot])
                o, new_hgrn[gi] = _gla(q, lf, v, g, b, l, p["hgrn_gnorm"][slot].reshape(1, HEAD_DIM),
                                       s_hgrn[gi], new_hgrn[gi], slot, n_hgrn)
                x2 = _proj_res(o, p["hgrn_w_out"], slot, x2, b, l, mod, 2)
                xs[gi] = _ffn(x2, b, l, mod, nw2, p["ffn_w_in"], p["ffn_w_out"], slot)
        else:
            lam_re, lam_im, bre, bim, cre, cim = p["s5"][slot]
            n_state = lam_re.shape[-1]
            for gi in range(n_groups):
                b, l = dims[gi]
                if s_re[gi] is None:
                    x0r = x0i = jnp.zeros((b, n_state), F32)
                else:
                    x0r = s_re[gi][slot].reshape(b, n_state)
                    x0i = s_im[gi][slot].reshape(b, n_state)
                xs[gi], fr, fi = _s5(xs[gi], b, l, layer_mods[gi], nw1, x0r, x0i, lam_re, lam_im, bre, bim, cre,
                                     cim, p["ssm_d"][slot].reshape(1, d), p["ssm_w_glu"], slot)
                new_re[gi].append(fr)
                new_im[gi].append(fi)
            xs = _moe(xs, dims, layer_mods, nw2, p["moe_w_router"][slot], p["moe_b_router"][slot],
                      p["moe_w_in"], p["moe_w_out"], slot)
    ys = [_final_norm(x2, p["final_norm_w"].reshape(1, d)) for x2 in xs]
    return ys, new_hgrn, [jnp.stack(r) for r in new_re], [jnp.stack(r) for r in new_im]


def kernel(x_prompt, x_sample, state_hgrn, state_ssm_re, state_ssm_im, c_prompt, c_sample, norm_w, ada_w, ada_b,
           hgrn_w_in, hgrn_lb, hgrn_gnorm, hgrn_w_out, ssm_a_re, ssm_a_im, ssm_log_dt, ssm_b_re, ssm_b_im,
           ssm_c_re, ssm_c_im, ssm_d, ssm_w_glu, ffn_w_in, ffn_w_out, moe_w_router, moe_b_router, moe_w_in,
           moe_w_out, final_norm_w):
    bp, lp, d = x_prompt.shape
    bs, ls, _ = x_sample.shape
    n_ssm, g, n_p = ssm_a_re.shape

    pr = jax.nn.softmax(hgrn_lb, axis=0)
    cs = jnp.cumsum(pr, axis=0)
    lb = cs - cs[0:1]
    p = {
        "norm_w": norm_w,
        "log_lb": jnp.log(lb)[:, None, :],
        "log1m_lb": jnp.log1p(-lb)[:, None, :],
        "hgrn_w_in": hgrn_w_in.astype(BF16),
        "hgrn_gnorm": hgrn_gnorm,
        "hgrn_w_out": hgrn_w_out.astype(BF16),
        "s5": [_s5_discretize(ssm_a_re[i], ssm_a_im[i], ssm_log_dt[i], ssm_b_re[i], ssm_b_im[i],
                              ssm_c_re[i], ssm_c_im[i]) for i in range(n_ssm)],
        "ssm_d": ssm_d,
        "ssm_w_glu": ssm_w_glu.astype(BF16),
        "ffn_w_in": ffn_w_in.astype(BF16),
        "ffn_w_out": ffn_w_out.astype(BF16),
        "moe_w_router": moe_w_router,
        "moe_b_router": moe_b_router,
        "moe_w_in": moe_w_in.astype(BF16),
        "moe_w_out": moe_w_out.astype(BF16),
        "final_norm_w": final_norm_w,
    }

    mods = _modulation(jnp.concatenate([c_prompt, c_sample], axis=0), ada_w, ada_b)
    mods_p = mods[:, :bp]
    mods_s = mods[:, bp:]

    (yp, ys), (hgrn_p, hgrn_s), (re_p, re_s), (im_p, im_s) = _trunk(
        [x_prompt.reshape(bp * lp, d), x_sample.reshape(bs * ls, d)], [(bp, lp), (bs, ls)], [mods_p, mods_s],
        [None, state_hgrn], [None, state_ssm_re], [None, state_ssm_im], p)

    def ssm_shape(a, b):
        return a.reshape(n_ssm, b, g, n_p)

    return (yp.reshape(bp, lp, d), ys.reshape(bs, ls, d),
            hgrn_p, ssm_shape(re_p, bp), ssm_shape(im_p, bp),
            hgrn_s, ssm_shape(re_s, bs), ssm_shape(im_s, bs))
```

```python
import functools
import math

import jax
import jax.numpy as jnp
from jax import lax
from jax.experimental import pallas as pl
from jax.experimental.pallas import tpu as pltpu

F32 = jnp.float32
BF16 = jnp.bfloat16
HIGHEST = lax.Precision.HIGHEST

NORM_EPS = 1e-6
LOG2_E = 1.4426950408889634
LANES = 128
SUBLANES = 8
BF16_ROWS = 16
ROW_TILE = 512
HEAD_DIM = 128
GLA_CHUNK = 64
GLA_HEADS_PER_STEP = 32
SSM_GROUP = 16
SSM_LANE_CHUNK = 1024
MOE_TILE = 768
VMEM_LIMIT = 56 * 1024 * 1024


def _params(*sem):
    return pltpu.CompilerParams(dimension_semantics=sem, vmem_limit_bytes=VMEM_LIMIT)


def _sigmoid(x):
    return 1.0 / (1.0 + jnp.exp(-x))


def _silu(x):
    return x * _sigmoid(x)


def _dot(a, b):
    return jnp.dot(a, b, preferred_element_type=F32)


def _tile_geometry(b, l):
    tl = min(l, ROW_TILE)
    bb = max(1, min(b, ROW_TILE // tl))
    assert l % tl == 0 and b % bb == 0
    return bb, tl


def _mod_rows(m_ref, b0, bb, tl):
    if bb == 1:
        return m_ref[pl.ds(b0, 1), :]
    rows = bb * tl
    n_b = m_ref.shape[0]
    r = lax.broadcasted_iota(jnp.int32, (rows, n_b), 0)
    c = lax.broadcasted_iota(jnp.int32, (rows, n_b), 1)
    expand = jnp.where(c == b0 + r // tl, 1.0, 0.0)
    return jnp.dot(expand, m_ref[...], preferred_element_type=F32, precision=HIGHEST)


def _norm_mod(x, nw, sc, sh):
    ms = jnp.mean(x * x, axis=-1, keepdims=True)
    return x * lax.rsqrt(ms + NORM_EPS) * nw * (1.0 + sc) + sh


def _row_specs(rows, d, n_b):
    x_spec = pl.BlockSpec((rows, d), lambda i, *_: (i, 0))

    def mod_spec(k):
        return pl.BlockSpec((n_b, d), lambda i, *_: (0, k))

    return x_spec, mod_spec


def _mod_kernel(c_ref, w_ref, b_ref, o_ref):
    c = c_ref[...]
    o_ref[0] = _dot(_silu(c).astype(BF16), w_ref[0].astype(BF16)) + b_ref[0]


def _modulation(c_all, ada_w, ada_b):
    depth, d, n = ada_w.shape
    rows = c_all.shape[0]
    tn = 1536
    assert n % tn == 0
    return pl.pallas_call(
        _mod_kernel,
        grid=(depth, n // tn),
        in_specs=[
            pl.BlockSpec((rows, d), lambda l, j: (0, 0)),
            pl.BlockSpec((1, d, tn), lambda l, j: (l, 0, j)),
            pl.BlockSpec((1, 1, tn), lambda l, j: (l, 0, j)),
        ],
        out_specs=pl.BlockSpec((1, rows, tn), lambda l, j: (l, 0, j)),
        out_shape=jax.ShapeDtypeStruct((depth, rows, n), F32),
        compiler_params=_params("arbitrary", "arbitrary"),
        name="adaln_modulation",
    )(c_all, ada_w, ada_b.reshape(depth, 1, n))


def _hgrn_in_kernel(x_ref, nw_ref, sh_ref, sc_ref, w_ref, la_ref, lc_ref, q_ref, lf_ref, v_ref, g_ref,
                    *, bb, tl, l_total):
    rows, d = x_ref.shape
    b0 = (pl.program_id(0) * rows) // l_total
    h = _norm_mod(x_ref[...], nw_ref[...], _mod_rows(sc_ref, b0, bb, tl), _mod_rows(sh_ref, b0, bb, tl))
    h = h.astype(BF16)
    q_ref[...] = _silu(_dot(h, w_ref[:, 0:d])).astype(q_ref.dtype)
    f = _dot(h, w_ref[:, d : 2 * d])
    log_sig = jnp.minimum(f, 0.0) - jnp.log1p(jnp.exp(-jnp.abs(f)))
    c = lc_ref[...] + log_sig
    a = la_ref[...]
    lf_ref[...] = jnp.maximum(a, c) + jnp.log1p(jnp.exp(-jnp.abs(a - c)))
    v_ref[...] = _dot(h, w_ref[:, 2 * d : 3 * d]).astype(v_ref.dtype)
    g_ref[...] = _silu(_dot(h, w_ref[:, 3 * d : 4 * d])).astype(g_ref.dtype)


def _hgrn_in(x2, b, l, mod, nw, w_in, slot, log_lb, log1m_lb):
    t, d = x2.shape
    bb, tl = _tile_geometry(b, l)
    rows = bb * tl
    x_spec, mod_spec = _row_specs(rows, d, b)
    row = pl.BlockSpec((1, d), lambda i: (0, 0))
    act = BF16 if l % BF16_ROWS == 0 else F32
    return pl.pallas_call(
        functools.partial(_hgrn_in_kernel, bb=bb, tl=tl, l_total=l),
        grid=(t // rows,),
        in_specs=[x_spec, row, mod_spec(0), mod_spec(1),
                  pl.BlockSpec((None, d, 4 * d), lambda i: (slot, 0, 0)), row, row],
        out_specs=[x_spec, x_spec, x_spec, x_spec],
        out_shape=[jax.ShapeDtypeStruct((t, d), act), jax.ShapeDtypeStruct((t, d), F32),
                   jax.ShapeDtypeStruct((t, d), act), jax.ShapeDtypeStruct((t, d), act)],
        compiler_params=_params("arbitrary"),
        name="hgrn_in_proj",
    )(x2, nw, mod, mod, w_in, log_lb, log1m_lb)


def _cumsum_rows(x):
    n = x.shape[0]
    row = lax.broadcasted_iota(jnp.int32, x.shape, 0)
    shift = 1
    while shift < n:
        x = x + jnp.where(row >= shift, pltpu.roll(x, shift, axis=0), 0.0)
        shift *= 2
    return x


def _gla_head(q, lf, v, g, gw, s, c):
    nb = c // SUBLANES
    lf2 = lf * LOG2_E
    b = _cumsum_rows(lf2)
    k = 1.0 - jnp.exp2(lf2)
    vb = v.astype(BF16)

    o = _dot((q * jnp.exp2(b)).astype(BF16), s.astype(BF16))

    b3 = b.reshape(nb, SUBLANES, HEAD_DIM)
    q3 = q.reshape(nb, SUBLANES, HEAD_DIM)
    k3 = k.reshape(nb, SUBLANES, HEAD_DIM)
    v3 = v.reshape(nb, SUBLANES, HEAD_DIM)
    t_in = lax.broadcasted_iota(jnp.int32, (nb, SUBLANES, HEAD_DIM), 1)
    o_diag = jnp.zeros((nb, SUBLANES, HEAD_DIM), F32)
    for s_in in range(SUBLANES):
        diff = b3 - b3[:, s_in : s_in + 1, :]
        e = jnp.exp2(jnp.where(t_in >= s_in, diff, -jnp.inf))
        w = jnp.sum(q3 * e * k3[:, s_in : s_in + 1, :], axis=-1, keepdims=True)
        o_diag = o_diag + w * v3[:, s_in : s_in + 1, :]
    o = o + o_diag.reshape(c, HEAD_DIM)

    if c > SUBLANES:
        row = lax.broadcasted_iota(jnp.int32, (c, c), 0)
        col = lax.broadcasted_iota(jnp.int32, (c, c), 1)
        t_row = lax.broadcasted_iota(jnp.int32, (c, HEAD_DIM), 0)
        a = jnp.zeros((c, c), F32)
        half = SUBLANES
        while half < c:
            span = 2 * half
            m = b.reshape(c // span, span, HEAD_DIM)[:, half - 1 : half, :]
            m = jnp.broadcast_to(m, (c // span, span, HEAD_DIM)).reshape(c, HEAD_DIM)
            right = (t_row & half) != 0
            qh = q * jnp.exp2(jnp.where(right, b - m, -jnp.inf))
            kh = k * jnp.exp2(jnp.where(right, -jnp.inf, m - b))
            sc = lax.dot_general(qh.astype(BF16), kh.astype(BF16), (((1,), (1,)), ((), ())),
                                 preferred_element_type=F32)
            shift = int(math.log2(span))
            a = a + jnp.where((row >> shift) == (col >> shift), sc, 0.0)
            half = span
        o = o + _dot(a.astype(BF16), vb)

    b_last = b[c - 1 : c, :]
    kd = k * jnp.exp2(b_last - b)
    pad = HEAD_DIM - c
    x = jnp.concatenate([kd, jnp.broadcast_to(jnp.exp2(b_last), (pad, HEAD_DIM))], axis=0)
    xt = x.T
    v_pad = jnp.concatenate([v, jnp.zeros((pad, HEAD_DIM), F32)], axis=0).astype(BF16)
    s_new = xt[:, c : c + 1] * s + _dot(xt.astype(BF16), v_pad)

    ms = jnp.mean(o * o, axis=-1, keepdims=True)
    return o * lax.rsqrt(ms + NORM_EPS) * gw * g, s_new


def _gla_kernel(*refs, c, heads, n_chunks, has_s0):
    q_ref, lf_ref, v_ref, g_ref, gw_ref = refs[:5]
    s0_ref = refs[5] if has_s0 else None
    o_ref, sf_ref, s_scr = refs[-3:]
    step = pl.program_id(1)

    @pl.when(step == 0)
    def _():
        if has_s0:
            s_scr[...] = s0_ref[...]
        else:
            s_scr[...] = jnp.zeros_like(s_scr)

    gw = gw_ref[...]
    for bi in range(q_ref.shape[0]):
        for h in range(heads):
            sl = slice(h * HEAD_DIM, (h + 1) * HEAD_DIM)
            out, s_new = _gla_head(q_ref[bi, :, sl].astype(F32), lf_ref[bi, :, sl], v_ref[bi, :, sl].astype(F32),
                                   g_ref[bi, :, sl].astype(F32), gw, s_scr[bi, h], c)
            o_ref[bi, :, sl] = out.astype(o_ref.dtype)
            s_scr[bi, h] = s_new

    @pl.when(step == n_chunks - 1)
    def _():
        sf_ref[...] = s_scr[...]


def _gla(q2, lf2, v2, g2, b, l, gnorm_w, s0_all, s_prev, slot, n_slots):
    d = q2.shape[-1]
    heads = d // HEAD_DIM
    c = math.gcd(l, GLA_CHUNK)
    n_chunks = l // c
    bt = max(1, min(b, GLA_HEADS_PER_STEP // heads // max(1, c // 32)))
    assert b % bt == 0
    has_s0 = s0_all is not None
    blk = pl.BlockSpec((bt, c, d), lambda i, j: (i, j, 0))
    s_spec = pl.BlockSpec((None, bt, heads, HEAD_DIM, HEAD_DIM), lambda i, j: (slot, i, 0, 0, 0))
    args = [a.reshape(b, l, d) for a in (q2, lf2, v2, g2)] + [gnorm_w]
    in_specs = [blk, blk, blk, blk, pl.BlockSpec((1, HEAD_DIM), lambda i, j: (0, 0))]
    if has_s0:
        args.append(s0_all)
        in_specs.append(s_spec)
    if s_prev is not None:
        args.append(s_prev)
        in_specs.append(pl.BlockSpec(memory_space=pl.ANY))
    o, s_fin = pl.pallas_call(
        functools.partial(_gla_kernel, c=c, heads=heads, n_chunks=n_chunks, has_s0=has_s0),
        grid=(b // bt, n_chunks),
        in_specs=in_specs,
        out_specs=[blk, s_spec],
        out_shape=[jax.ShapeDtypeStruct((b, l, d), q2.dtype),
                   jax.ShapeDtypeStruct((n_slots, b, heads, HEAD_DIM, HEAD_DIM), F32)],
        input_output_aliases={len(args) - 1: 1} if s_prev is not None else {},
        scratch_shapes=[pltpu.VMEM((bt, heads, HEAD_DIM, HEAD_DIM), F32)],
        compiler_params=_params("arbitrary", "arbitrary"),
        name="hgrn_gla_scan",
    )(*args)
    return o.reshape(b * l, d), s_fin


def _proj_res_kernel(a_ref, w_ref, x_ref, gt_ref, o_ref, *, bb, tl, l_total):
    rows = x_ref.shape[0]
    b0 = (pl.program_id(0) * rows) // l_total
    y = _dot(a_ref[...].astype(BF16), w_ref[...])
    o_ref[...] = x_ref[...] + _mod_rows(gt_ref, b0, bb, tl) * y


def _proj_res(a2, w, slot, x2, b, l, mod, gate_idx):
    t, d = x2.shape
    k = a2.shape[-1]
    bb, tl = _tile_geometry(b, l)
    rows = bb * tl
    x_spec, mod_spec = _row_specs(rows, d, b)
    return pl.pallas_call(
        functools.partial(_proj_res_kernel, bb=bb, tl=tl, l_total=l),
        grid=(t // rows,),
        in_specs=[pl.BlockSpec((rows, k), lambda i: (i, 0)), pl.BlockSpec((None, k, d), lambda i: (slot, 0, 0)),
                  x_spec, mod_spec(gate_idx)],
        out_specs=x_spec,
        out_shape=jax.ShapeDtypeStruct((t, d), F32),
        compiler_params=_params("arbitrary"),
        name="proj_residual",
    )(a2, w, x2, mod)


def _ff_tile(d_ff):
    best = LANES
    for t in range(LANES, 1408 + 1, LANES):
        if d_ff % t == 0:
            best = t
    return best


def _ffn_kernel(x_ref, nw_ref, sh_ref, sc_ref, gt_ref, wg_ref, wu_ref, wo_ref, o_ref, h_scr, acc_scr,
                *, nf, bb, tl, l_total):
    rows = x_ref.shape[0]
    b0 = (pl.program_id(0) * rows) // l_total
    j = pl.program_id(1)

    @pl.when(j == 0)
    def _():
        h = _norm_mod(x_ref[...], nw_ref[...], _mod_rows(sc_ref, b0, bb, tl), _mod_rows(sh_ref, b0, bb, tl))
        h_scr[...] = h.astype(BF16)
        acc_scr[...] = jnp.zeros_like(acc_scr)

    h = h_scr[...]
    act = (_silu(_dot(h, wg_ref[...])) * _dot(h, wu_ref[...])).astype(BF16)
    acc_scr[...] += _dot(act, wo_ref[...])

    @pl.when(j == nf - 1)
    def _():
        o_ref[...] = x_ref[...] + _mod_rows(gt_ref, b0, bb, tl) * acc_scr[...]


def _ffn(x2, b, l, mod, nw, w_in, w_out, slot):
    t, d = x2.shape
    d_ff = w_out.shape[-2]
    tf = _ff_tile(d_ff)
    nf = d_ff // tf
    bb, tl = _tile_geometry(b, l)
    rows = bb * tl
    x_spec, mod_spec = _row_specs(rows, d, b)
    return pl.pallas_call(
        functools.partial(_ffn_kernel, nf=nf, bb=bb, tl=tl, l_total=l),
        grid=(t // rows, nf),
        in_specs=[
            x_spec, pl.BlockSpec((1, d), lambda i, j: (0, 0)), mod_spec(3), mod_spec(4), mod_spec(5),
            pl.BlockSpec((None, d, tf), lambda i, j: (slot, 0, j)),
            pl.BlockSpec((None, d, tf), lambda i, j: (slot, 0, nf + j)),
            pl.BlockSpec((None, tf, d), lambda i, j: (slot, j, 0)),
        ],
        out_specs=x_spec,
        out_shape=jax.ShapeDtypeStruct((t, d), F32),
        scratch_shapes=[pltpu.VMEM((rows, d), BF16), pltpu.VMEM((rows, d), F32)],
        compiler_params=_params("arbitrary", "arbitrary"),
        name="dense_swiglu",
    )(x2, nw, mod, mod, mod, w_in, w_in, w_out)


def _route_kernel(x_ref, nw_ref, sh_ref, sc_ref, wr_ref, br_ref, cnt0_ref, xl_ref, slot_ref, prob_ref, n_ref,
                  off_ref, base_ref, cnt_ref, cnt_scr, *, n_steps, bb, tl, l_total):
    rows, d = x_ref.shape
    step = pl.program_id(0)
    b0 = (step * rows) // l_total

    @pl.when(step == 0)
    def _():
        cnt_scr[...] = cnt0_ref[...]

    h = _norm_mod(x_ref[...], nw_ref[...], _mod_rows(sc_ref, b0, bb, tl), _mod_rows(sh_ref, b0, bb, tl))
    logits = jnp.dot(h, wr_ref[...], preferred_element_type=F32, precision=HIGHEST) + br_ref[...]
    lane = lax.broadcasted_iota(jnp.int32, (rows, LANES), 1)
    m0 = jnp.max(logits, axis=-1, keepdims=True)
    i0 = jnp.min(jnp.where(logits == m0, lane, LANES), axis=-1, keepdims=True)
    rest = jnp.where(lane == i0, -jnp.inf, logits)
    m1 = jnp.max(rest, axis=-1, keepdims=True)
    i1 = jnp.min(jnp.where(rest == m1, lane, LANES), axis=-1, keepdims=True)
    e1 = jnp.exp(m1 - m0)
    p0 = 1.0 / (1.0 + e1)
    p1 = e1 / (1.0 + e1)
    hit0 = lane == i0
    hit1 = lane == i1
    onehot = jnp.where(hit0 | hit1, 1.0, 0.0)
    r_i = lax.broadcasted_iota(jnp.int32, (rows, rows), 0)
    c_i = lax.broadcasted_iota(jnp.int32, (rows, rows), 1)
    before = _dot((r_i > c_i).astype(BF16), onehot.astype(BF16))
    n_e = jnp.sum(onehot, axis=0, keepdims=True)
    n_e = jnp.ceil(n_e * (1.0 / SUBLANES)) * SUBLANES
    e_i = lax.broadcasted_iota(jnp.int32, (LANES, LANES), 0)
    e_j = lax.broadcasted_iota(jnp.int32, (LANES, LANES), 1)
    off_e = jnp.dot(jnp.broadcast_to(n_e, (SUBLANES, LANES)), jnp.where(e_i < e_j, 1.0, 0.0),
                    preferred_element_type=F32, precision=HIGHEST)[0:1]
    slot0 = jnp.sum(jnp.where(hit0, before + off_e, 0.0), axis=-1, keepdims=True)
    slot1 = jnp.sum(jnp.where(hit1, before + off_e, 0.0), axis=-1, keepdims=True)
    slot0_row = jnp.broadcast_to(slot0, (rows, LANES)).T[0:1, :]
    slot1_row = jnp.broadcast_to(slot1, (rows, LANES)).T[0:1, :]
    s_i = lax.broadcasted_iota(jnp.int32, (xl_ref.shape[0], rows), 0).astype(F32)
    perm = jnp.where((s_i == slot0_row) | (s_i == slot1_row), 1.0, 0.0).astype(BF16)
    xl_ref[...] = _dot(perm, h.astype(BF16))
    slot_ref[...] = jnp.where(lane == 0, slot0, jnp.where(lane == 1, slot1, 0.0)).astype(jnp.int32)
    prob_ref[...] = jnp.where(lane == 0, p0, jnp.where(lane == 1, p1, 0.0))
    n_ref[0] = n_e.astype(jnp.int32)
    off_ref[0] = off_e.astype(jnp.int32)
    base_ref[0] = cnt_scr[...].astype(jnp.int32)
    cnt_scr[...] += n_e

    @pl.when(step == n_steps - 1)
    def _():
        cnt_ref[...] = cnt_scr[...]


def _route(x2, b, l, mod, nw, wr, br, cnt0):
    t, d = x2.shape
    bb, tl = _tile_geometry(b, l)
    rows = bb * tl
    n_steps = t // rows
    x_spec, mod_spec = _row_specs(rows, d, b)
    row = pl.BlockSpec((1, LANES), lambda i: (0, 0))
    lanes_spec = pl.BlockSpec((rows, LANES), lambda i: (i, 0))
    meta_spec = pl.BlockSpec((1, 1, LANES), lambda i: (i, 0, 0))
    meta_shape = jax.ShapeDtypeStruct((n_steps, 1, LANES), jnp.int32)
    return pl.pallas_call(
        functools.partial(_route_kernel, n_steps=n_steps, bb=bb, tl=tl, l_total=l),
        grid=(n_steps,),
        in_specs=[x_spec, pl.BlockSpec((1, d), lambda i: (0, 0)), mod_spec(3), mod_spec(4),
                  pl.BlockSpec((d, LANES), lambda i: (0, 0)), row, row],
        out_specs=[pl.BlockSpec((_local_rows(rows), d), lambda i: (i, 0)), lanes_spec, lanes_spec,
                   meta_spec, meta_spec, meta_spec, row],
        out_shape=[jax.ShapeDtypeStruct((n_steps * _local_rows(rows), d), F32),
                   jax.ShapeDtypeStruct((t, LANES), jnp.int32),
                   jax.ShapeDtypeStruct((t, LANES), F32), meta_shape, meta_shape, meta_shape,
                   jax.ShapeDtypeStruct((1, LANES), F32)],
        scratch_shapes=[pltpu.VMEM((1, LANES), F32)],
        compiler_params=_params("arbitrary"),
        name="moe_route",
    )(x2, nw, mod, mod, wr, br, cnt0)


def _row_copy(src, src_row, dst, dst_row, sem):
    return pltpu.make_async_copy(src.at[pl.ds(src_row, 1)], dst.at[pl.ds(dst_row, 1)], sem)


def _local_rows(rows):
    return 2 * rows + LANES


def _run_pieces(n, max_rows, fn):
    off = 0
    for bit in reversed(range(SUBLANES.bit_length() - 1, max_rows.bit_length())):
        size = 1 << bit
        take = (n & size) != 0

        @pl.when(take)
        def _(off=off, size=size):
            fn(off, size)

        off = off + jnp.where(take, size, 0)


def _run_copies(n_ref, off_ref, base_ref, n_exp, cap, max_rows, local, local_row0, routed, sem, to_routed):
    for wait in (False, True):
        for e in range(n_exp):
            lo = local_row0 + off_ref[0, 0, e]
            hi = e * cap + base_ref[0, 0, e]

            def piece(o, size, lo=lo, hi=hi):
                a = local.at[pl.ds(pl.multiple_of(lo + o, SUBLANES), size)]
                b = routed.at[pl.ds(pl.multiple_of(hi + o, SUBLANES), size)]
                cp = pltpu.make_async_copy(a, b, sem) if to_routed else pltpu.make_async_copy(b, a, sem)
                if wait:
                    cp.wait()
                else:
                    cp.start()

            _run_pieces(n_ref[0, 0, e], max_rows, piece)


def _scatter_kernel(cnt_ref, n_ref, off_ref, base_ref, xl_ref, *rest, rows, cap, n_steps, fill):
    if fill:
        _, xg_ref, zero_row, sem = rest
    else:
        xg_ref, zero_row, sem = rest
    n_exp = cnt_ref.shape[0]
    _run_copies(n_ref, off_ref, base_ref, n_exp, cap, rows, xl_ref, pl.program_id(0) * _local_rows(rows), xg_ref,
                sem, to_routed=True)

    if fill:
        @pl.when(pl.program_id(0) == n_steps - 1)
        def _():
            zero_row[...] = jnp.zeros_like(zero_row)
            for e in range(cnt_ref.shape[0]):
                cnt = cnt_ref[e]
                n_pad = lax.rem(MOE_TILE - lax.rem(cnt, MOE_TILE), MOE_TILE)

                def put(r, carry, base=e * cap + cnt):
                    _row_copy(zero_row, 0, xg_ref, base + r, sem).start()
                    return carry

                def done(r, carry):
                    _row_copy(zero_row, 0, xg_ref, 0, sem).wait()
                    return carry

                lax.fori_loop(0, n_pad, put, 0)
                lax.fori_loop(0, n_pad, done, 0)


def _scatter(counts, meta, xl, rows, xg, n_slots, cap):
    d = xl.shape[-1]
    n_steps = xl.shape[0] // _local_rows(rows)
    fill = xg is not None
    meta_spec = pl.BlockSpec((1, 1, LANES), lambda i, c: (i, 0, 0), memory_space=pltpu.SMEM)
    in_specs = [meta_spec, meta_spec, meta_spec, pl.BlockSpec(memory_space=pl.ANY)]
    args = [counts, *meta, xl]
    if fill:
        in_specs.append(pl.BlockSpec(memory_space=pl.ANY))
        args.append(xg)
    return pl.pallas_call(
        functools.partial(_scatter_kernel, rows=rows, cap=cap, n_steps=n_steps, fill=fill),
        grid_spec=pltpu.PrefetchScalarGridSpec(
            num_scalar_prefetch=1, grid=(n_steps,), in_specs=in_specs,
            out_specs=pl.BlockSpec(memory_space=pl.ANY),
            scratch_shapes=[pltpu.VMEM((1, d), F32), pltpu.SemaphoreType.DMA(())]),
        out_shape=jax.ShapeDtypeStruct((n_slots, d), F32),
        input_output_aliases={5: 0} if fill else {},
        compiler_params=_params("arbitrary"),
        name="moe_scatter",
    )(*args)


def _gmm_kernel(blk_ref, exp_ref, used_ref, x_ref, wg_ref, wu_ref, wo_ref, o_ref, xb_scr):
    i = pl.program_id(0)
    j = pl.program_id(1)

    @pl.when(i < used_ref[0])
    def _():
        @pl.when(j == 0)
        def _():
            xb_scr[...] = x_ref[...].astype(BF16)

        h = xb_scr[...]
        act = (_silu(_dot(h, wg_ref[...])) * _dot(h, wu_ref[...])).astype(BF16)
        y = _dot(act, wo_ref[...])

        @pl.when(j == 0)
        def _():
            o_ref[...] = y

        @pl.when(j > 0)
        def _():
            o_ref[...] += y


def _gmm(tile_blk, tile_exp, n_used, xg, w_in, w_out, slot):
    n_slots, d = xg.shape
    d_ff = w_out.shape[-2]
    tf = _ff_tile(d_ff)
    nf = d_ff // tf
    n_tiles = tile_blk.shape[0]

    def ff(i, j, used):
        return jnp.where(i < used[0], j, nf - 1)

    row_spec = pl.BlockSpec((MOE_TILE, d), lambda i, j, blk, ex, used: (blk[i], 0))
    return pl.pallas_call(
        _gmm_kernel,
        grid_spec=pltpu.PrefetchScalarGridSpec(
            num_scalar_prefetch=3, grid=(n_tiles, nf),
            in_specs=[
                row_spec,
                pl.BlockSpec((None, None, d, tf), lambda i, j, blk, ex, used: (slot, ex[i], 0, ff(i, j, used))),
                pl.BlockSpec((None, None, d, tf),
                             lambda i, j, blk, ex, used: (slot, ex[i], 0, nf + ff(i, j, used))),
                pl.BlockSpec((None, None, tf, d), lambda i, j, blk, ex, used: (slot, ex[i], ff(i, j, used), 0)),
            ],
            out_specs=row_spec,
            scratch_shapes=[pltpu.VMEM((MOE_TILE, d), BF16)]),
        out_shape=jax.ShapeDtypeStruct((n_slots, d), F32),
        compiler_params=_params("arbitrary", "arbitrary"),
        name="moe_grouped_swiglu",
    )(tile_blk, tile_exp, n_used, xg, w_in, w_in, w_out)


def _combine_kernel(n_ref, off_ref, base_ref, x_ref, gt_ref, slot_ref, prob_ref, y_ref, o_ref, yl, sem,
                    *, bb, tl, l_total, n_exp, cap):
    rows = x_ref.shape[0]
    b0 = (pl.program_id(0) * rows) // l_total
    yl[2 * rows :, :] = jnp.zeros((yl.shape[0] - 2 * rows, yl.shape[1]), F32)
    _run_copies(n_ref, off_ref, base_ref, n_exp, cap, rows, yl, 0, y_ref, sem, to_routed=False)
    y_bf = yl[...].astype(BF16)
    s_j = lax.broadcasted_iota(jnp.int32, (rows, yl.shape[0]), 1)
    slots = slot_ref[...]
    pick0 = jnp.where(s_j == slots[:, 0:1], 1.0, 0.0).astype(BF16)
    pick1 = jnp.where(s_j == slots[:, 1:2], 1.0, 0.0).astype(BF16)
    prob = prob_ref[...]
    f = prob[:, 0:1] * _dot(pick0, y_bf) + prob[:, 1:2] * _dot(pick1, y_bf)
    o_ref[...] = x_ref[...] + _mod_rows(gt_ref, b0, bb, tl) * f


def _combine(meta, slots, x2, b, l, mod, prob, y, n_exp, cap):
    t, d = x2.shape
    bb, tl = _tile_geometry(b, l)
    rows = bb * tl
    x_spec, mod_spec = _row_specs(rows, d, b)
    meta_spec = pl.BlockSpec((1, 1, LANES), lambda i: (i, 0, 0), memory_space=pltpu.SMEM)
    lanes_spec = pl.BlockSpec((rows, LANES), lambda i: (i, 0))
    return pl.pallas_call(
        functools.partial(_combine_kernel, bb=bb, tl=tl, l_total=l, n_exp=n_exp, cap=cap),
        grid=(t // rows,),
        in_specs=[meta_spec, meta_spec, meta_spec, x_spec, mod_spec(5), lanes_spec, lanes_spec,
                  pl.BlockSpec(memory_space=pl.ANY)],
        out_specs=x_spec,
        out_shape=jax.ShapeDtypeStruct((t, d), F32),
        scratch_shapes=[pltpu.VMEM((_local_rows(rows), d), F32), pltpu.SemaphoreType.DMA(())],
        compiler_params=_params("arbitrary"),
        name="moe_combine",
    )(*meta, x2, mod, slots, prob, y)


def _moe(xs, dims, mods, nw, w_router, b_router, w_in, w_out, slot):
    d = xs[0].shape[-1]
    n_exp = w_in.shape[1]
    t_total = sum(x.shape[0] for x in xs)
    n_row_tiles = sum(x.shape[0] // math.prod(_tile_geometry(b, l)) for x, (b, l) in zip(xs, dims))
    run_pad = (SUBLANES - 1) * n_row_tiles
    cap = -(-(t_total + run_pad) // MOE_TILE) * MOE_TILE
    n_slots = n_exp * cap
    n_tiles = -(-(2 * t_total + n_exp * (run_pad + MOE_TILE - 1)) // MOE_TILE)
    wr = jnp.zeros((d, LANES), F32).at[:, :n_exp].set(w_router)
    br = jnp.full((1, LANES), -jnp.inf, F32).at[0, :n_exp].set(b_router)

    cnt = jnp.zeros((1, LANES), F32)
    routed = []
    for x2, (b, l), mod in zip(xs, dims, mods):
        xl, slots, prob, n_e, off_e, base_e, cnt = _route(x2, b, l, mod, nw, wr, br, cnt)
        routed.append((xl, slots, prob, (n_e, off_e, base_e)))
    counts = cnt[0, :n_exp].astype(jnp.int32)

    xg = None
    for (xl, _, _, meta), (b, l) in zip(routed, dims):
        bb, tl = _tile_geometry(b, l)
        xg = _scatter(counts, meta, xl, bb * tl, xg, n_slots, cap)

    tiles = (counts + (MOE_TILE - 1)) // MOE_TILE
    ends = jnp.cumsum(tiles)
    n_used = ends[-1:]
    idx = jnp.minimum(jnp.arange(n_tiles, dtype=jnp.int32), n_used - 1)
    tile_exp = jnp.sum((idx[:, None] >= ends[None, :]).astype(jnp.int32), axis=1)
    tile_blk = tile_exp * (cap // MOE_TILE) + idx - (ends - tiles)[tile_exp]
    y = _gmm(tile_blk.astype(jnp.int32), tile_exp, n_used.astype(jnp.int32), xg, w_in, w_out, slot)

    return [_combine(meta, slots, x2, b, l, mod, prob, y, n_exp, cap)
            for x2, (b, l), mod, (_, slots, prob, meta) in zip(xs, dims, mods, routed)]


def _gelu_tanh(x):
    return 0.5 * x * (1.0 + jnp.tanh(math.sqrt(2.0 / math.pi) * (x + 0.044715 * (x * x * x))))


def _s5_kernel(x_ref, nw_ref, sh_ref, sc_ref, gt_ref, bre_ref, bim_ref, cre_ref, cim_ref, lr_ref, li_ref,
               dsk_ref, wglu_ref, x0r_ref, x0i_ref, o_ref, fr_ref, fi_ref,
               u_scr, bur, bui, y_scr, sr, si, to_tm, to_bm, *, n_steps):
    bb, tc, d = x_ref.shape
    rows = bb * tc
    n_lane_chunks = bre_ref.shape[0]
    in_chunk = bre_ref.shape[1]
    half_lanes = SSM_LANE_CHUNK // 2
    b0 = pl.program_id(0) * bb
    t = pl.program_id(1)

    @pl.when(t == 0)
    def _():
        sr[...] = x0r_ref[...]
        si[...] = x0i_ref[...]
        r_i = lax.broadcasted_iota(jnp.int32, (rows, rows), 0)
        c_i = lax.broadcasted_iota(jnp.int32, (rows, rows), 1)
        bb_bits, tc_bits = bb.bit_length() - 1, tc.bit_length() - 1
        to_tm[...] = (c_i == ((r_i & (bb - 1)) << tc_bits) + (r_i >> bb_bits)).astype(BF16)
        to_bm[...] = (c_i == ((r_i & (tc - 1)) << bb_bits) + (r_i >> tc_bits)).astype(BF16)

    x = x_ref[...].reshape(rows, d)
    h = _norm_mod(x, nw_ref[...], _mod_rows(sc_ref, b0, bb, tc), _mod_rows(sh_ref, b0, bb, tc))
    h_hi = h.astype(BF16)
    h_lo = (h - h_hi.astype(F32)).astype(BF16)
    u_scr[...] = _dot(to_tm[...], h_hi) + _dot(to_tm[...], h_lo)

    for n in range(n_lane_chunks):
        u = u_scr[:, n * in_chunk : (n + 1) * in_chunk].astype(BF16)
        bur[...] = _dot(u, bre_ref[n])
        bui[...] = _dot(u, bim_ref[n])
        for hf in range(2):
            ls = slice(hf * half_lanes, (hf + 1) * half_lanes)
            g0 = n * SSM_LANE_CHUNK + hf * half_lanes
            lam_r = jnp.broadcast_to(lr_ref[:, g0 : g0 + half_lanes], (SUBLANES, half_lanes))
            lam_i = jnp.broadcast_to(li_ref[:, g0 : g0 + half_lanes], (SUBLANES, half_lanes))

            def batch_tile(bt, _):
                r0 = pl.multiple_of(bt * SUBLANES, SUBLANES)

                def step(l, carry):
                    xr, xi = carry
                    row = pl.multiple_of(l * bb + r0, SUBLANES)
                    nr = lam_r * xr - lam_i * xi + bur[pl.ds(row, SUBLANES), ls]
                    ni = lam_r * xi + lam_i * xr + bui[pl.ds(row, SUBLANES), ls]
                    bur[pl.ds(row, SUBLANES), ls] = nr
                    bui[pl.ds(row, SUBLANES), ls] = ni
                    return nr, ni

                xr, xi = lax.fori_loop(
                    0, tc, step,
                    (sr[pl.ds(r0, SUBLANES), g0 : g0 + half_lanes], si[pl.ds(r0, SUBLANES), g0 : g0 + half_lanes]),
                    unroll=8)
                sr[pl.ds(r0, SUBLANES), g0 : g0 + half_lanes] = xr
                si[pl.ds(r0, SUBLANES), g0 : g0 + half_lanes] = xi
                return 0

            lax.fori_loop(0, bb // SUBLANES, batch_tile, 0)
        y_scr[:, n * in_chunk : (n + 1) * in_chunk] = (
            _dot(bur[...].astype(BF16), cre_ref[n]) - _dot(bui[...].astype(BF16), cim_ref[n]))

    y = y_scr[...] + dsk_ref[...] * u_scr[...]
    z = _dot(to_bm[...], _gelu_tanh(y).astype(BF16)).astype(BF16)
    zz = _dot(z, wglu_ref[...])
    m = zz[:, :d] * _sigmoid(zz[:, d:])
    o_ref[...] = (x + _mod_rows(gt_ref, b0, bb, tc) * m).reshape(bb, tc, d)

    @pl.when(t == n_steps - 1)
    def _():
        fr_ref[...] = sr[...]
        fi_ref[...] = si[...]


def _s5(x2, b, l, mod, nw, x0_re, x0_im, lam_re, lam_im, b_re_blk, b_im_blk, c_re_blk, c_im_blk, d_skip, w_glu,
        slot):
    t, d = x2.shape
    n_state = lam_re.shape[-1]
    if b * l <= ROW_TILE:
        bb, tc = b, l
    elif b <= ROW_TILE // SUBLANES:
        bb, tc = b, ROW_TILE // b
    else:
        tc = l
        bb = ROW_TILE // l
    assert l % tc == 0 and b % bb == 0 and bb % SUBLANES == 0 and tc % SUBLANES == 0
    assert bb & (bb - 1) == 0 and tc & (tc - 1) == 0
    n_steps = l // tc
    rows = tc * bb
    x_spec = pl.BlockSpec((bb, tc, d), lambda i, s: (i, s, 0))

    def mod_spec(k):
        return pl.BlockSpec((b, d), lambda i, s: (0, k))

    def full(a):
        nd = a.ndim
        return pl.BlockSpec(a.shape, lambda i, s: (0,) * nd)

    st_spec = pl.BlockSpec((bb, n_state), lambda i, s: (i, 0))
    out, fr, fi = pl.pallas_call(
        functools.partial(_s5_kernel, n_steps=n_steps),
        grid=(b // bb, n_steps),
        in_specs=[x_spec, full(nw), mod_spec(0), mod_spec(1), mod_spec(2), full(b_re_blk), full(b_im_blk),
                  full(c_re_blk), full(c_im_blk), full(lam_re), full(lam_im), full(d_skip),
                  pl.BlockSpec((None,) + w_glu.shape[1:], lambda i, s: (slot, 0, 0)),
                  st_spec, st_spec],
        out_specs=[x_spec, st_spec, st_spec],
        out_shape=[jax.ShapeDtypeStruct((b, l, d), F32), jax.ShapeDtypeStruct((b, n_state), F32),
                   jax.ShapeDtypeStruct((b, n_state), F32)],
        scratch_shapes=[pltpu.VMEM((rows, d), F32), pltpu.VMEM((rows, SSM_LANE_CHUNK), F32),
                        pltpu.VMEM((rows, SSM_LANE_CHUNK), F32), pltpu.VMEM((rows, d), F32),
                        pltpu.VMEM((bb, n_state), F32), pltpu.VMEM((bb, n_state), F32),
                        pltpu.VMEM((rows, rows), BF16), pltpu.VMEM((rows, rows), BF16)],
        compiler_params=_params("arbitrary", "arbitrary"),
        name="s5_mixer",
    )(x2.reshape(b, l, d), nw, mod, mod, mod, b_re_blk, b_im_blk, c_re_blk, c_im_blk, lam_re, lam_im, d_skip,
      w_glu, x0_re, x0_im)
    return out.reshape(t, d), fr, fi


def _s5_discretize(a_re, a_im, log_dt, b_re, b_im, c_re, c_im):
    g, p = a_re.shape
    dt = jnp.exp(log_dt)[:, None]
    mag = jnp.exp(a_re * dt)
    lam_re = mag * jnp.cos(a_im * dt)
    lam_im = mag * jnp.sin(a_im * dt)
    den = a_re * a_re + a_im * a_im
    coef_re = ((lam_re - 1.0) * a_re + lam_im * a_im) / den
    coef_im = (lam_im * a_re - (lam_re - 1.0) * a_im) / den
    bb_re = coef_re[..., None] * b_re - coef_im[..., None] * b_im
    bb_im = coef_re[..., None] * b_im + coef_im[..., None] * b_re
    gpb = SSM_LANE_CHUNK // p
    nblk = g // gpb
    eye = jnp.eye(gpb, dtype=F32)

    def in_map(bb):
        t = bb.reshape(nblk, gpb, p, SSM_GROUP)
        return jnp.einsum("ngpc,gh->ngchp", t, eye).reshape(nblk, gpb * SSM_GROUP, gpb * p).astype(BF16)

    def out_map(c):
        t = c.reshape(nblk, gpb, SSM_GROUP, p)
        return jnp.einsum("ngcp,gh->ngphc", t, eye).reshape(nblk, gpb * p, gpb * SSM_GROUP).astype(BF16)

    return (lam_re.reshape(1, g * p), lam_im.reshape(1, g * p), in_map(bb_re), in_map(bb_im),
            out_map(c_re), out_map(c_im))


def _final_norm_kernel(x_ref, w_ref, o_ref):
    x = x_ref[...]
    ms = jnp.mean(x * x, axis=-1, keepdims=True)
    o_ref[...] = x * lax.rsqrt(ms + NORM_EPS) * w_ref[...]


def _final_norm(x2, w):
    t, d = x2.shape
    rows = min(ROW_TILE, t)
    x_spec = pl.BlockSpec((rows, d), lambda i: (i, 0))
    return pl.pallas_call(
        _final_norm_kernel,
        grid=(t // rows,),
        in_specs=[x_spec, pl.BlockSpec((1, d), lambda i: (0, 0))],
        out_specs=x_spec,
        out_shape=jax.ShapeDtypeStruct((t, d), F32),
        compiler_params=_params("arbitrary"),
        name="final_norm",
    )(x2, w)


def _trunk(xs, dims, mods, s_hgrn, s_re, s_im, p):
    depth = p["norm_w"].shape[0]
    n_groups = len(xs)
    d = xs[0].shape[-1]
    n_hgrn = (depth + 1) // 2
    new_hgrn = [None] * n_groups
    new_re = [[] for _ in xs]
    new_im = [[] for _ in xs]
    for layer in range(depth):
        slot = layer // 2
        nw1 = p["norm_w"][layer, 0].reshape(1, d)
        nw2 = p["norm_w"][layer, 1].reshape(1, d)
        layer_mods = [m[layer] for m in mods]
        if layer % 2 == 0:
            for gi in range(n_groups):
                x2, (b, l), mod = xs[gi], dims[gi], layer_mods[gi]
                q, lf, v, g = _hgrn_in(x2, b, l, mod, nw1, p["hgrn_w_in"], slot, p["log_lb"][slot],
                                       p["log1m_lb"][slot])
                o, new_hgrn[gi] = _gla(q, lf, v, g, b, l, p["hgrn_gnorm"][slot].reshape(1, HEAD_DIM),
                                       s_hgrn[gi], new_hgrn[gi], slot, n_hgrn)
                x2 = _proj_res(o, p["hgrn_w_out"], slot, x2, b, l, mod, 2)
                xs[gi] = _ffn(x2, b, l, mod, nw2, p["ffn_w_in"], p["ffn_w_out"], slot)
        else:
            lam_re, lam_im, bre, bim, cre, cim = p["s5"][slot]
            n_state = lam_re.shape[-1]
            for gi in range(n_groups):
                b, l = dims[gi]
                if s_re[gi] is None:
                    x0r = x0i = jnp.zeros((b, n_state), F32)
                else:
                    x0r = s_re[gi][slot].reshape(b, n_state)
                    x0i = s_im[gi][slot].reshape(b, n_state)
                xs[gi], fr, fi = _s5(xs[gi], b, l, layer_mods[gi], nw1, x0r, x0i, lam_re, lam_im, bre, bim, cre,
                                     cim, p["ssm_d"][slot].reshape(1, d), p["ssm_w_glu"], slot)
                new_re[gi].append(fr)
                new_im[gi].append(fi)
            xs = _moe(xs, dims, layer_mods, nw2, p["moe_w_router"][slot], p["moe_b_router"][slot],
                      p["moe_w_in"], p["moe_w_out"], slot)
    ys = [_final_norm(x2, p["final_norm_w"].reshape(1, d)) for x2 in xs]
    return ys, new_hgrn, [jnp.stack(r) for r in new_re], [jnp.stack(r) for r in new_im]


def kernel(x_prompt, x_sample, state_hgrn, state_ssm_re, state_ssm_im, c_prompt, c_sample, norm_w, ada_w, ada_b,
           hgrn_w_in, hgrn_lb, hgrn_gnorm, hgrn_w_out, ssm_a_re, ssm_a_im, ssm_log_dt, ssm_b_re, ssm_b_im,
           ssm_c_re, ssm_c_im, ssm_d, ssm_w_glu, ffn_w_in, ffn_w_out, moe_w_router, moe_b_router, moe_w_in,
           moe_w_out, final_norm_w):
    bp, lp, d = x_prompt.shape
    bs, ls, _ = x_sample.shape
    n_ssm, g, n_p = ssm_a_re.shape

    pr = jax.nn.softmax(hgrn_lb, axis=0)
    cs = jnp.cumsum(pr, axis=0)
    lb = cs - cs[0:1]
    p = {
        "norm_w": norm_w,
        "log_lb": jnp.log(lb)[:, None, :],
        "log1m_lb": jnp.log1p(-lb)[:, None, :],
        "hgrn_w_in": hgrn_w_in.astype(BF16),
        "hgrn_gnorm": hgrn_gnorm,
        "hgrn_w_out": hgrn_w_out.astype(BF16),
        "s5": [_s5_discretize(ssm_a_re[i], ssm_a_im[i], ssm_log_dt[i], ssm_b_re[i], ssm_b_im[i],
                              ssm_c_re[i], ssm_c_im[i]) for i in range(n_ssm)],
        "ssm_d": ssm_d,
        "ssm_w_glu": ssm_w_glu.astype(BF16),
        "ffn_w_in": ffn_w_in.astype(BF16),
        "ffn_w_out": ffn_w_out.astype(BF16),
        "moe_w_router": moe_w_router,
        "moe_b_router": moe_b_router,
        "moe_w_in": moe_w_in.astype(BF16),
        "moe_w_out": moe_w_out.astype(BF16),
        "final_norm_w": final_norm_w,
    }

    mods = _modulation(jnp.concatenate([c_prompt, c_sample], axis=0), ada_w, ada_b)
    mods_p = mods[:, :bp]
    mods_s = mods[:, bp:]

    (yp, ys), (hgrn_p, hgrn_s), (re_p, re_s), (im_p, im_s) = _trunk(
        [x_prompt.reshape(bp * lp, d), x_sample.reshape(bs * ls, d)], [(bp, lp), (bs, ls)], [mods_p, mods_s],
        [None, state_hgrn], [None, state_ssm_re], [None, state_ssm_im], p)

    def ssm_shape(a, b):
        return a.reshape(n_ssm, b, g, n_p)

    return (yp.reshape(bp, lp, d), ys.reshape(bs, ls, d),
            hgrn_p, ssm_shape(re_p, bp), ssm_shape(im_p, bp),
            hgrn_s, ssm_shape(re_s, bs), ssm_shape(im_s, bs))
```

```python
import functools
import math

import jax
import jax.numpy as jnp
from jax import lax
from jax.experimental import pallas as pl
from jax.experimental.pallas import tpu as pltpu

F32 = jnp.float32
BF16 = jnp.bfloat16
HIGHEST = lax.Precision.HIGHEST

NORM_EPS = 1e-6
LOG2_E = 1.4426950408889634
LANES = 128
SUBLANES = 8
BF16_ROWS = 16
ROW_TILE = 512
HEAD_DIM = 128
HGRN_IN_COLS = 256
GLA_CHUNK = 64
GLA_HEADS_PER_STEP = 32
SSM_GROUP = 16
SSM_LANE_CHUNK = 1024
MOE_TILE = 768
DMA_UNROLL = 8
VMEM_LIMIT = 56 * 1024 * 1024


def _params(*sem):
    return pltpu.CompilerParams(dimension_semantics=sem, vmem_limit_bytes=VMEM_LIMIT)


def _sigmoid(x):
    return 1.0 / (1.0 + jnp.exp(-x))


def _silu(x):
    return x * _sigmoid(x)


def _dot(a, b):
    return jnp.dot(a, b, preferred_element_type=F32)


def _tile_geometry(b, l):
    tl = min(l, ROW_TILE)
    bb = max(1, min(b, ROW_TILE // tl))
    assert l % tl == 0 and b % bb == 0
    return bb, tl


def _mod_rows(m_ref, b0, bb, tl):
    if bb == 1:
        return m_ref[pl.ds(b0, 1), :]
    if bb % SUBLANES == 0:
        b0 = pl.multiple_of(b0, SUBLANES)
    m = m_ref[pl.ds(b0, bb), :]
    return jnp.concatenate([jnp.broadcast_to(m[i : i + 1, :], (tl, m.shape[1])) for i in range(bb)], axis=0)


def _norm_mod(x, nw, sc, sh):
    ms = jnp.mean(x * x, axis=-1, keepdims=True)
    return x * lax.rsqrt(ms + NORM_EPS) * nw * (1.0 + sc) + sh


def _row_specs(rows, d, n_b):
    x_spec = pl.BlockSpec((rows, d), lambda i, *_: (i, 0))

    def mod_spec(k):
        return pl.BlockSpec((n_b, d), lambda i, *_: (0, k))

    return x_spec, mod_spec


def _mod_kernel(c_ref, w_ref, b_ref, o_ref):
    c = c_ref[...]
    o_ref[0] = _dot(_silu(c).astype(BF16), w_ref[0].astype(BF16)) + b_ref[0]


def _modulation(c_all, ada_w, ada_b):
    depth, d, n = ada_w.shape
    rows = c_all.shape[0]
    tn = 1536
    assert n % tn == 0
    return pl.pallas_call(
        _mod_kernel,
        grid=(depth, n // tn),
        in_specs=[
            pl.BlockSpec((rows, d), lambda l, j: (0, 0)),
            pl.BlockSpec((1, d, tn), lambda l, j: (l, 0, j)),
            pl.BlockSpec((1, 1, tn), lambda l, j: (l, 0, j)),
        ],
        out_specs=pl.BlockSpec((1, rows, tn), lambda l, j: (l, 0, j)),
        out_shape=jax.ShapeDtypeStruct((depth, rows, n), F32),
        compiler_params=_params("arbitrary", "arbitrary"),
        name="adaln_modulation",
    )(c_all, ada_w, ada_b.reshape(depth, 1, n))


def _hgrn_in_kernel(x_ref, nw_ref, sh_ref, sc_ref, w_ref, la_ref, lc_ref, q_ref, lf_ref, v_ref, g_ref,
                    *, bb, tl, l_total):
    rows, d = x_ref.shape
    b0 = (pl.program_id(0) * rows) // l_total
    h = _norm_mod(x_ref[...], nw_ref[...], _mod_rows(sc_ref, b0, bb, tl), _mod_rows(sh_ref, b0, bb, tl))
    h = h.astype(BF16)
    for c0 in range(0, d, HGRN_IN_COLS):
        cs = slice(c0, c0 + HGRN_IN_COLS)
        q_ref[:, cs] = _silu(_dot(h, w_ref[:, c0 : c0 + HGRN_IN_COLS])).astype(q_ref.dtype)
        f = _dot(h, w_ref[:, d + c0 : d + c0 + HGRN_IN_COLS])
        log_sig = jnp.minimum(f, 0.0) - jnp.log1p(jnp.exp(-jnp.abs(f)))
        c = lc_ref[:, cs] + log_sig
        a = la_ref[:, cs]
        lf_ref[:, cs] = jnp.maximum(a, c) + jnp.log1p(jnp.exp(-jnp.abs(a - c)))
        v_ref[:, cs] = _dot(h, w_ref[:, 2 * d + c0 : 2 * d + c0 + HGRN_IN_COLS]).astype(v_ref.dtype)
        g_ref[:, cs] = _silu(_dot(h, w_ref[:, 3 * d + c0 : 3 * d + c0 + HGRN_IN_COLS])).astype(g_ref.dtype)


def _hgrn_in(x2, b, l, mod, nw, w_in, slot, log_lb, log1m_lb):
    t, d = x2.shape
    bb, tl = _tile_geometry(b, l)
    rows = bb * tl
    x_spec, mod_spec = _row_specs(rows, d, b)
    row = pl.BlockSpec((1, d), lambda i: (0, 0))
    act = BF16 if l % BF16_ROWS == 0 else F32
    return pl.pallas_call(
        functools.partial(_hgrn_in_kernel, bb=bb, tl=tl, l_total=l),
        grid=(t // rows,),
        in_specs=[x_spec, row, mod_spec(0), mod_spec(1),
                  pl.BlockSpec((None, d, 4 * d), lambda i: (slot, 0, 0)), row, row],
        out_specs=[x_spec, x_spec, x_spec, x_spec],
        out_shape=[jax.ShapeDtypeStruct((t, d), act), jax.ShapeDtypeStruct((t, d), F32),
                   jax.ShapeDtypeStruct((t, d), act), jax.ShapeDtypeStruct((t, d), act)],
        compiler_params=_params("arbitrary"),
        name="hgrn_in_proj",
    )(x2, nw, mod, mod, w_in, log_lb, log1m_lb)


def _cumsum_rows(x):
    n = x.shape[0]
    row = lax.broadcasted_iota(jnp.int32, x.shape, 0)
    shift = 1
    while shift < n:
        x = x + jnp.where(row >= shift, pltpu.roll(x, shift, axis=0), 0.0)
        shift *= 2
    return x


def _gla_head(q, lf, v, g, gw, s, c):
    nb = c // SUBLANES
    lf2 = lf * LOG2_E
    b = _cumsum_rows(lf2)
    k = 1.0 - jnp.exp2(lf2)
    vb = v.astype(BF16)

    o = _dot((q * jnp.exp2(b)).astype(BF16), s.astype(BF16))

    b3 = b.reshape(nb, SUBLANES, HEAD_DIM)
    q3 = q.reshape(nb, SUBLANES, HEAD_DIM)
    k3 = k.reshape(nb, SUBLANES, HEAD_DIM)
    v3 = v.reshape(nb, SUBLANES, HEAD_DIM)
    t_in = lax.broadcasted_iota(jnp.int32, (nb, SUBLANES, HEAD_DIM), 1)
    o_diag = jnp.zeros((nb, SUBLANES, HEAD_DIM), F32)
    for s_in in range(SUBLANES):
        diff = b3 - b3[:, s_in : s_in + 1, :]
        e = jnp.exp2(jnp.where(t_in >= s_in, diff, -jnp.inf))
        w = jnp.sum(q3 * e * k3[:, s_in : s_in + 1, :], axis=-1, keepdims=True)
        o_diag = o_diag + w * v3[:, s_in : s_in + 1, :]
    o = o + o_diag.reshape(c, HEAD_DIM)

    if c > SUBLANES:
        row = lax.broadcasted_iota(jnp.int32, (c, c), 0)
        col = lax.broadcasted_iota(jnp.int32, (c, c), 1)
        t_row = lax.broadcasted_iota(jnp.int32, (c, HEAD_DIM), 0)
        a = jnp.zeros((c, c), F32)
        half = SUBLANES
        while half < c:
            span = 2 * half
            m = b.reshape(c // span, span, HEAD_DIM)[:, half - 1 : half, :]
            m = jnp.broadcast_to(m, (c // span, span, HEAD_DIM)).reshape(c, HEAD_DIM)
            right = (t_row & half) != 0
            qh = q * jnp.exp2(jnp.where(right, b - m, -jnp.inf))
            kh = k * jnp.exp2(jnp.where(right, -jnp.inf, m - b))
            sc = lax.dot_general(qh.astype(BF16), kh.astype(BF16), (((1,), (1,)), ((), ())),
                                 preferred_element_type=F32)
            shift = int(math.log2(span))
            a = a + jnp.where((row >> shift) == (col >> shift), sc, 0.0)
            half = span
        o = o + _dot(a.astype(BF16), vb)

    b_last = b[c - 1 : c, :]
    kd = k * jnp.exp2(b_last - b)
    pad = HEAD_DIM - c
    x = jnp.concatenate([kd, jnp.broadcast_to(jnp.exp2(b_last), (pad, HEAD_DIM))], axis=0)
    xt = x.T
    v_pad = jnp.concatenate([v, jnp.zeros((pad, HEAD_DIM), F32)], axis=0).astype(BF16)
    s_new = xt[:, c : c + 1] * s + _dot(xt.astype(BF16), v_pad)

    ms = jnp.mean(o * o, axis=-1, keepdims=True)
    return o * lax.rsqrt(ms + NORM_EPS) * gw * g, s_new


def _gla_kernel(*refs, c, heads, n_chunks, has_s0):
    q_ref, lf_ref, v_ref, g_ref, gw_ref = refs[:5]
    s0_ref = refs[5] if has_s0 else None
    o_ref, sf_ref, s_scr = refs[-3:]
    step = pl.program_id(1)

    @pl.when(step == 0)
    def _():
        if has_s0:
            s_scr[...] = s0_ref[...]
        else:
            s_scr[...] = jnp.zeros_like(s_scr)

    gw = gw_ref[...]
    for bi in range(q_ref.shape[0]):
        for h in range(heads):
            sl = slice(h * HEAD_DIM, (h + 1) * HEAD_DIM)
            out, s_new = _gla_head(q_ref[bi, :, sl].astype(F32), lf_ref[bi, :, sl], v_ref[bi, :, sl].astype(F32),
                                   g_ref[bi, :, sl].astype(F32), gw, s_scr[bi, h], c)
            o_ref[bi, :, sl] = out.astype(o_ref.dtype)
            s_scr[bi, h] = s_new

    @pl.when(step == n_chunks - 1)
    def _():
        sf_ref[...] = s_scr[...]


def _gla(q2, lf2, v2, g2, b, l, gnorm_w, s0_all, s_prev, slot, n_slots):
    d = q2.shape[-1]
    heads = d // HEAD_DIM
    c = math.gcd(l, GLA_CHUNK)
    n_chunks = l // c
    bt = max(1, min(b, GLA_HEADS_PER_STEP // heads // max(1, c // 32)))
    assert b % bt == 0
    has_s0 = s0_all is not None
    blk = pl.BlockSpec((bt, c, d), lambda i, j: (i, j, 0))
    s_spec = pl.BlockSpec((None, bt, heads, HEAD_DIM, HEAD_DIM), lambda i, j: (slot, i, 0, 0, 0))
    args = [a.reshape(b, l, d) for a in (q2, lf2, v2, g2)] + [gnorm_w]
    in_specs = [blk, blk, blk, blk, pl.BlockSpec((1, HEAD_DIM), lambda i, j: (0, 0))]
    if has_s0:
        args.append(s0_all)
        in_specs.append(s_spec)
    if s_prev is not None:
        args.append(s_prev)
        in_specs.append(pl.BlockSpec(memory_space=pl.ANY))
    o, s_fin = pl.pallas_call(
        functools.partial(_gla_kernel, c=c, heads=heads, n_chunks=n_chunks, has_s0=has_s0),
        grid=(b // bt, n_chunks),
        in_specs=in_specs,
        out_specs=[blk, s_spec],
        out_shape=[jax.ShapeDtypeStruct((b, l, d), q2.dtype),
                   jax.ShapeDtypeStruct((n_slots, b, heads, HEAD_DIM, HEAD_DIM), F32)],
        input_output_aliases={len(args) - 1: 1} if s_prev is not None else {},
        scratch_shapes=[pltpu.VMEM((bt, heads, HEAD_DIM, HEAD_DIM), F32)],
        compiler_params=_params("arbitrary", "arbitrary"),
        name="hgrn_gla_scan",
    )(*args)
    return o.reshape(b * l, d), s_fin


def _ff_tile(d_ff):
    best = LANES
    for t in range(LANES, 1408 + 1, LANES):
        if d_ff % t == 0:
            best = t
    return best


def _ffn_kernel(a_ref, wp_ref, g1_ref, x_ref, nw_ref, sh_ref, sc_ref, gt_ref, wg_ref, wu_ref, wo_ref, o_ref,
                x1_scr, h_scr, acc_scr, *, nf, bb, tl, l_total):
    rows = x_ref.shape[0]
    b0 = (pl.program_id(0) * rows) // l_total
    j = pl.program_id(1)

    @pl.when(j == 0)
    def _():
        x1 = x_ref[...] + _mod_rows(g1_ref, b0, bb, tl) * _dot(a_ref[...].astype(BF16), wp_ref[...])
        x1_scr[...] = x1
        h = _norm_mod(x1, nw_ref[...], _mod_rows(sc_ref, b0, bb, tl), _mod_rows(sh_ref, b0, bb, tl))
        h_scr[...] = h.astype(BF16)
        acc_scr[...] = jnp.zeros_like(acc_scr)

    h = h_scr[...]
    act = (_silu(_dot(h, wg_ref[...])) * _dot(h, wu_ref[...])).astype(BF16)
    acc_scr[...] += _dot(act, wo_ref[...])

    @pl.when(j == nf - 1)
    def _():
        o_ref[...] = x1_scr[...] + _mod_rows(gt_ref, b0, bb, tl) * acc_scr[...]


def _proj_ffn(a2, w_proj, x2, b, l, mod, nw, w_in, w_out, slot):
    t, d = x2.shape
    k = a2.shape[-1]
    d_ff = w_out.shape[-2]
    tf = _ff_tile(d_ff)
    nf = d_ff // tf
    bb, tl = _tile_geometry(b, l)
    rows = bb * tl
    x_spec, mod_spec = _row_specs(rows, d, b)
    return pl.pallas_call(
        functools.partial(_ffn_kernel, nf=nf, bb=bb, tl=tl, l_total=l),
        grid=(t // rows, nf),
        in_specs=[
            pl.BlockSpec((rows, k), lambda i, j: (i, 0)), pl.BlockSpec((None, k, d), lambda i, j: (slot, 0, 0)),
            mod_spec(2),
            x_spec, pl.BlockSpec((1, d), lambda i, j: (0, 0)), mod_spec(3), mod_spec(4), mod_spec(5),
            pl.BlockSpec((None, d, tf), lambda i, j: (slot, 0, j)),
            pl.BlockSpec((None, d, tf), lambda i, j: (slot, 0, nf + j)),
            pl.BlockSpec((None, tf, d), lambda i, j: (slot, j, 0)),
        ],
        out_specs=x_spec,
        out_shape=jax.ShapeDtypeStruct((t, d), F32),
        scratch_shapes=[pltpu.VMEM((rows, d), F32), pltpu.VMEM((rows, d), BF16), pltpu.VMEM((rows, d), F32)],
        compiler_params=_params("arbitrary", "arbitrary"),
        name="proj_dense_swiglu",
    )(a2, w_proj, mod, x2, nw, mod, mod, mod, w_in, w_in, w_out)


def _route_kernel(x_ref, nw_ref, sh_ref, sc_ref, wr_ref, br_ref, cnt0_ref, h_ref, pos_ref, prob_ref, cnt_ref,
                  cnt_scr, *, cap, n_steps, bb, tl, l_total):
    rows, d = x_ref.shape
    step = pl.program_id(0)
    b0 = (step * rows) // l_total

    @pl.when(step == 0)
    def _():
        cnt_scr[...] = cnt0_ref[...]

    h = _norm_mod(x_ref[...], nw_ref[...], _mod_rows(sc_ref, b0, bb, tl), _mod_rows(sh_ref, b0, bb, tl))
    h_ref[...] = h
    logits = jnp.dot(h, wr_ref[...], preferred_element_type=F32, precision=HIGHEST) + br_ref[...]
    lane = lax.broadcasted_iota(jnp.int32, (rows, LANES), 1)
    m0 = jnp.max(logits, axis=-1, keepdims=True)
    i0 = jnp.min(jnp.where(logits == m0, lane, LANES), axis=-1, keepdims=True)
    rest = jnp.where(lane == i0, -jnp.inf, logits)
    m1 = jnp.max(rest, axis=-1, keepdims=True)
    i1 = jnp.min(jnp.where(rest == m1, lane, LANES), axis=-1, keepdims=True)
    e1 = jnp.exp(m1 - m0)
    p0 = 1.0 / (1.0 + e1)
    p1 = e1 / (1.0 + e1)
    hit0 = lane == i0
    hit1 = lane == i1
    onehot = jnp.where(hit0 | hit1, 1.0, 0.0)
    r_i = lax.broadcasted_iota(jnp.int32, (rows, rows), 0)
    c_i = lax.broadcasted_iota(jnp.int32, (rows, rows), 1)
    before = _dot((r_i > c_i).astype(BF16), onehot.astype(BF16)) + cnt_scr[...]
    rank0 = jnp.sum(jnp.where(hit0, before, 0.0), axis=-1, keepdims=True).astype(jnp.int32)
    rank1 = jnp.sum(jnp.where(hit1, before, 0.0), axis=-1, keepdims=True).astype(jnp.int32)
    pos_ref[...] = jnp.where(lane == 0, i0 * cap + rank0, jnp.where(lane == 1, i1 * cap + rank1, 0))
    prob_ref[...] = jnp.where(lane == 0, p0, jnp.where(lane == 1, p1, 0.0))
    cnt_scr[...] += jnp.sum(onehot, axis=0, keepdims=True)

    @pl.when(step == n_steps - 1)
    def _():
        cnt_ref[...] = cnt_scr[...]


def _route(x2, b, l, mod, nw, wr, br, cnt0, cap):
    t, d = x2.shape
    bb, tl = _tile_geometry(b, l)
    rows = bb * tl
    n_steps = t // rows
    x_spec, mod_spec = _row_specs(rows, d, b)
    row = pl.BlockSpec((1, LANES), lambda i: (0, 0))
    lanes_spec = pl.BlockSpec((rows, LANES), lambda i: (i, 0))
    return pl.pallas_call(
        functools.partial(_route_kernel, cap=cap, n_steps=n_steps, bb=bb, tl=tl, l_total=l),
        grid=(n_steps,),
        in_specs=[x_spec, pl.BlockSpec((1, d), lambda i: (0, 0)), mod_spec(3), mod_spec(4),
                  pl.BlockSpec((d, LANES), lambda i: (0, 0)), row, row],
        out_specs=[x_spec, lanes_spec, lanes_spec, row],
        out_shape=[jax.ShapeDtypeStruct((t, d), F32), jax.ShapeDtypeStruct((t, LANES), jnp.int32),
                   jax.ShapeDtypeStruct((t, LANES), F32), jax.ShapeDtypeStruct((1, LANES), F32)],
        scratch_shapes=[pltpu.VMEM((1, LANES), F32)],
        compiler_params=_params("arbitrary"),
        name="moe_route",
    )(x2, nw, mod, mod, wr, br, cnt0)


def _row_copy(src, src_row, dst, dst_row, sem):
    return pltpu.make_async_copy(src.at[pl.ds(src_row, 1)], dst.at[pl.ds(dst_row, 1)], sem)


def _scatter_kernel(cnt_ref, pos_ref, h_ref, *rest, rows, cap, n_steps, fill):
    if fill:
        _, xg_ref, zero_row, sem = rest
    else:
        xg_ref, zero_row, sem = rest

    def issue(it, carry):
        for u in range(DMA_UNROLL):
            r = it * DMA_UNROLL + u
            _row_copy(h_ref, r, xg_ref, pos_ref[0, 0, 2 * r], sem).start(priority=0)
            _row_copy(h_ref, r, xg_ref, pos_ref[0, 0, 2 * r + 1], sem).start(priority=1)
        return carry

    lax.fori_loop(0, rows // DMA_UNROLL, issue, 0)
    for _ in range(2):
        pltpu.make_async_copy(h_ref, xg_ref.at[pl.ds(0, rows)], sem).wait()

    if fill:
        @pl.when(pl.program_id(0) == n_steps - 1)
        def _():
            zero_row[...] = jnp.zeros_like(zero_row)
            for e in range(cnt_ref.shape[0]):
                cnt = cnt_ref[e]
                n_pad = lax.rem(MOE_TILE - lax.rem(cnt, MOE_TILE), MOE_TILE)

                def put(r, carry, base=e * cap + cnt):
                    _row_copy(zero_row, 0, xg_ref, base + r, sem).start()
                    return carry

                def done(r, carry):
                    _row_copy(zero_row, 0, xg_ref, 0, sem).wait()
                    return carry

                lax.fori_loop(0, n_pad, put, 0)
                lax.fori_loop(0, n_pad, done, 0)


def _scatter(counts, pos, h2, xg, n_slots, cap):
    t, d = h2.shape
    rows = min(ROW_TILE, t)
    assert rows % DMA_UNROLL == 0
    n_steps = t // rows
    fill = xg is not None
    in_specs = [pl.BlockSpec((1, 1, 2 * rows), lambda i, c: (i, 0, 0), memory_space=pltpu.SMEM),
                pl.BlockSpec((rows, d), lambda i, c: (i, 0))]
    args = [counts, pos, h2]
    if fill:
        in_specs.append(pl.BlockSpec(memory_space=pl.ANY))
        args.append(xg)
    return pl.pallas_call(
        functools.partial(_scatter_kernel, rows=rows, cap=cap, n_steps=n_steps, fill=fill),
        grid_spec=pltpu.PrefetchScalarGridSpec(
            num_scalar_prefetch=1, grid=(n_steps,), in_specs=in_specs,
            out_specs=pl.BlockSpec(memory_space=pl.ANY),
            scratch_shapes=[pltpu.VMEM((1, d), F32), pltpu.SemaphoreType.DMA(())]),
        out_shape=jax.ShapeDtypeStruct((n_slots, d), F32),
        input_output_aliases={3: 0} if fill else {},
        compiler_params=_params("arbitrary"),
        name="moe_scatter",
    )(*args)


def _gmm_kernel(blk_ref, exp_ref, used_ref, x_ref, wg_ref, wu_ref, wo_ref, o_ref, xb_scr):
    i = pl.program_id(0)
    j = pl.program_id(1)

    @pl.when(i < used_ref[0])
    def _():
        @pl.when(j == 0)
        def _():
            xb_scr[...] = x_ref[...].astype(BF16)

        h = xb_scr[...]
        act = (_silu(_dot(h, wg_ref[...])) * _dot(h, wu_ref[...])).astype(BF16)
        y = _dot(act, wo_ref[...])

        @pl.when(j == 0)
        def _():
            o_ref[...] = y

        @pl.when(j > 0)
        def _():
            o_ref[...] += y


def _gmm(tile_blk, tile_exp, n_used, xg, w_in, w_out, slot):
    n_slots, d = xg.shape
    d_ff = w_out.shape[-2]
    tf = _ff_tile(d_ff)
    nf = d_ff // tf
    n_tiles = tile_blk.shape[0]

    def ff(i, j, used):
        return jnp.where(i < used[0], j, nf - 1)

    row_spec = pl.BlockSpec((MOE_TILE, d), lambda i, j, blk, ex, used: (blk[i], 0))
    return pl.pallas_call(
        _gmm_kernel,
        grid_spec=pltpu.PrefetchScalarGridSpec(
            num_scalar_prefetch=3, grid=(n_tiles, nf),
            in_specs=[
                row_spec,
                pl.BlockSpec((None, None, d, tf), lambda i, j, blk, ex, used: (slot, ex[i], 0, ff(i, j, used))),
                pl.BlockSpec((None, None, d, tf),
                             lambda i, j, blk, ex, used: (slot, ex[i], 0, nf + ff(i, j, used))),
                pl.BlockSpec((None, None, tf, d), lambda i, j, blk, ex, used: (slot, ex[i], ff(i, j, used), 0)),
            ],
            out_specs=row_spec,
            scratch_shapes=[pltpu.VMEM((MOE_TILE, d), BF16)]),
        out_shape=jax.ShapeDtypeStruct((n_slots, d), F32),
        compiler_params=_params("arbitrary", "arbitrary"),
        name="moe_grouped_swiglu",
    )(tile_blk, tile_exp, n_used, xg, w_in, w_in, w_out)


def _combine_kernel(pos_ref, x_ref, gt_ref, prob_ref, y_ref, *rest, bb, tl, l_total, final):
    fw_ref = rest[0] if final else None
    o_ref, buf0, buf1, sem = rest[-4:]
    rows = x_ref.shape[0]
    b0 = (pl.program_id(0) * rows) // l_total

    def issue(it, carry):
        for u in range(DMA_UNROLL):
            r = it * DMA_UNROLL + u
            _row_copy(y_ref, pos_ref[0, 0, 2 * r], buf0, r, sem).start(priority=0)
            _row_copy(y_ref, pos_ref[0, 0, 2 * r + 1], buf1, r, sem).start(priority=1)
        return carry

    lax.fori_loop(0, rows // DMA_UNROLL, issue, 0)
    for buf in (buf0, buf1):
        pltpu.make_async_copy(y_ref.at[pl.ds(0, rows)], buf, sem).wait()
    prob = prob_ref[...]
    f = prob[:, 0:1] * buf0[...] + prob[:, 1:2] * buf1[...]
    x = x_ref[...] + _mod_rows(gt_ref, b0, bb, tl) * f
    if final:
        ms = jnp.mean(x * x, axis=-1, keepdims=True)
        x = x * lax.rsqrt(ms + NORM_EPS) * fw_ref[...]
    o_ref[...] = x


def _combine(pos, x2, b, l, mod, prob, y, final_w):
    t, d = x2.shape
    bb, tl = _tile_geometry(b, l)
    rows = bb * tl
    assert rows % DMA_UNROLL == 0
    x_spec, mod_spec = _row_specs(rows, d, b)
    final = final_w is not None
    in_specs = [pl.BlockSpec((1, 1, 2 * rows), lambda i: (i, 0, 0), memory_space=pltpu.SMEM),
                x_spec, mod_spec(5), pl.BlockSpec((rows, LANES), lambda i: (i, 0)),
                pl.BlockSpec(memory_space=pl.ANY)]
    args = [pos, x2, mod, prob, y]
    if final:
        in_specs.append(pl.BlockSpec((1, d), lambda i: (0, 0)))
        args.append(final_w)
    return pl.pallas_call(
        functools.partial(_combine_kernel, bb=bb, tl=tl, l_total=l, final=final),
        grid=(t // rows,),
        in_specs=in_specs,
        out_specs=x_spec,
        out_shape=jax.ShapeDtypeStruct((t, d), F32),
        scratch_shapes=[pltpu.VMEM((rows, d), F32), pltpu.VMEM((rows, d), F32), pltpu.SemaphoreType.DMA(())],
        compiler_params=_params("arbitrary"),
        name="moe_combine",
    )(*args)


def _moe(xs, dims, mods, nw, w_router, b_router, w_in, w_out, slot, final_w=None):
    d = xs[0].shape[-1]
    n_exp = w_in.shape[1]
    t_total = sum(x.shape[0] for x in xs)
    cap = -(-t_total // MOE_TILE) * MOE_TILE
    n_slots = n_exp * cap
    n_tiles = -(-(2 * t_total + n_exp * (MOE_TILE - 1)) // MOE_TILE)
    wr = jnp.zeros((d, LANES), F32).at[:, :n_exp].set(w_router)
    br = jnp.full((1, LANES), -jnp.inf, F32).at[0, :n_exp].set(b_router)

    cnt = jnp.zeros((1, LANES), F32)
    routed = []
    for x2, (b, l), mod in zip(xs, dims, mods):
        h2, pos, prob, cnt = _route(x2, b, l, mod, nw, wr, br, cnt, cap)
        rows = min(ROW_TILE, pos.shape[0])
        routed.append((h2, pos[:, :2].reshape(pos.shape[0] // rows, 1, 2 * rows), prob))
    counts = cnt[0, :n_exp].astype(jnp.int32)

    xg = None
    for h2, pos, _ in routed:
        xg = _scatter(counts, pos, h2, xg, n_slots, cap)

    tiles = (counts + (MOE_TILE - 1)) // MOE_TILE
    ends = jnp.cumsum(tiles)
    n_used = ends[-1:]
    idx = jnp.minimum(jnp.arange(n_tiles, dtype=jnp.int32), n_used - 1)
    tile_exp = jnp.sum((idx[:, None] >= ends[None, :]).astype(jnp.int32), axis=1)
    tile_blk = tile_exp * (cap // MOE_TILE) + idx - (ends - tiles)[tile_exp]
    y = _gmm(tile_blk.astype(jnp.int32), tile_exp, n_used.astype(jnp.int32), xg, w_in, w_out, slot)

    return [_combine(pos, x2, b, l, mod, prob, y, final_w)
            for x2, (b, l), mod, (_, pos, prob) in zip(xs, dims, mods, routed)]


def _gelu_tanh(x):
    return 0.5 * x * (1.0 + jnp.tanh(math.sqrt(2.0 / math.pi) * (x + 0.044715 * (x * x * x))))


def _s5_kernel(x_ref, nw_ref, sh_ref, sc_ref, gt_ref, bre_ref, bim_ref, cre_ref, cim_ref, lr_ref, li_ref,
               dsk_ref, wglu_ref, x0r_ref, x0i_ref, o_ref, fr_ref, fi_ref,
               u_scr, bur, bui, y_scr, sr, si, to_tm, to_bm, *, n_steps):
    bb, tc, d = x_ref.shape
    rows = bb * tc
    n_lane_chunks = bre_ref.shape[0]
    in_chunk = bre_ref.shape[1]
    half_lanes = SSM_LANE_CHUNK // 2
    b0 = pl.program_id(0) * bb
    t = pl.program_id(1)

    @pl.when(t == 0)
    def _():
        sr[...] = x0r_ref[...]
        si[...] = x0i_ref[...]
        r_i = lax.broadcasted_iota(jnp.int32, (rows, rows), 0)
        c_i = lax.broadcasted_iota(jnp.int32, (rows, rows), 1)
        bb_bits, tc_bits = bb.bit_length() - 1, tc.bit_length() - 1
        to_tm[...] = (c_i == ((r_i & (bb - 1)) << tc_bits) + (r_i >> bb_bits)).astype(BF16)
        to_bm[...] = (c_i == ((r_i & (tc - 1)) << bb_bits) + (r_i >> tc_bits)).astype(BF16)

    x = x_ref[...].reshape(rows, d)
    h = _norm_mod(x, nw_ref[...], _mod_rows(sc_ref, b0, bb, tc), _mod_rows(sh_ref, b0, bb, tc))
    h_hi = h.astype(BF16)
    h_lo = (h - h_hi.astype(F32)).astype(BF16)
    u_scr[...] = _dot(to_tm[...], h_hi) + _dot(to_tm[...], h_lo)

    for n in range(n_lane_chunks):
        u = u_scr[:, n * in_chunk : (n + 1) * in_chunk].astype(BF16)
        bur[...] = _dot(u, bre_ref[n])
        bui[...] = _dot(u, bim_ref[n])
        for hf in range(2):
            ls = slice(hf * half_lanes, (hf + 1) * half_lanes)
            g0 = n * SSM_LANE_CHUNK + hf * half_lanes
            lam_r = jnp.broadcast_to(lr_ref[:, g0 : g0 + half_lanes], (SUBLANES, half_lanes))
            lam_i = jnp.broadcast_to(li_ref[:, g0 : g0 + half_lanes], (SUBLANES, half_lanes))

            def batch_tile(bt, _):
                r0 = pl.multiple_of(bt * SUBLANES, SUBLANES)

                def step(l, carry):
                    xr, xi = carry
                    row = pl.multiple_of(l * bb + r0, SUBLANES)
                    nr = lam_r * xr - lam_i * xi + bur[pl.ds(row, SUBLANES), ls]
                    ni = lam_r * xi + lam_i * xr + bui[pl.ds(row, SUBLANES), ls]
                    bur[pl.ds(row, SUBLANES), ls] = nr
                    bui[pl.ds(row, SUBLANES), ls] = ni
                    return nr, ni

                xr, xi = lax.fori_loop(
                    0, tc, step,
                    (sr[pl.ds(r0, SUBLANES), g0 : g0 + half_lanes], si[pl.ds(r0, SUBLANES), g0 : g0 + half_lanes]),
                    unroll=8)
                sr[pl.ds(r0, SUBLANES), g0 : g0 + half_lanes] = xr
                si[pl.ds(r0, SUBLANES), g0 : g0 + half_lanes] = xi
                return 0

            lax.fori_loop(0, bb // SUBLANES, batch_tile, 0)
        y_scr[:, n * in_chunk : (n + 1) * in_chunk] = (
            _dot(bur[...].astype(BF16), cre_ref[n]) - _dot(bui[...].astype(BF16), cim_ref[n]))

    y = y_scr[...] + dsk_ref[...] * u_scr[...]
    z = _dot(to_bm[...], _gelu_tanh(y).astype(BF16)).astype(BF16)
    zz = _dot(z, wglu_ref[...])
    m = zz[:, :d] * _sigmoid(zz[:, d:])
    o_ref[...] = (x + _mod_rows(gt_ref, b0, bb, tc) * m).reshape(bb, tc, d)

    @pl.when(t == n_steps - 1)
    def _():
        fr_ref[...] = sr[...]
        fi_ref[...] = si[...]


def _s5(x2, b, l, mod, nw, x0_re, x0_im, lam_re, lam_im, b_re_blk, b_im_blk, c_re_blk, c_im_blk, d_skip, w_glu,
        slot):
    t, d = x2.shape
    n_state = lam_re.shape[-1]
    if b * l <= ROW_TILE:
        bb, tc = b, l
    elif b <= ROW_TILE // SUBLANES:
        bb, tc = b, ROW_TILE // b
    else:
        tc = l
        bb = ROW_TILE // l
    assert l % tc == 0 and b % bb == 0 and bb % SUBLANES == 0 and tc % SUBLANES == 0
    assert bb & (bb - 1) == 0 and tc & (tc - 1) == 0
    n_steps = l // tc
    rows = tc * bb
    x_spec = pl.BlockSpec((bb, tc, d), lambda i, s: (i, s, 0))

    def mod_spec(k):
        return pl.BlockSpec((b, d), lambda i, s: (0, k))

    def full(a):
        nd = a.ndim
        return pl.BlockSpec(a.shape, lambda i, s: (0,) * nd)

    st_spec = pl.BlockSpec((bb, n_state), lambda i, s: (i, 0))
    out, fr, fi = pl.pallas_call(
        functools.partial(_s5_kernel, n_steps=n_steps),
        grid=(b // bb, n_steps),
        in_specs=[x_spec, full(nw), mod_spec(0), mod_spec(1), mod_spec(2), full(b_re_blk), full(b_im_blk),
                  full(c_re_blk), full(c_im_blk), full(lam_re), full(lam_im), full(d_skip),
                  pl.BlockSpec((None,) + w_glu.shape[1:], lambda i, s: (slot, 0, 0)),
                  st_spec, st_spec],
        out_specs=[x_spec, st_spec, st_spec],
        out_shape=[jax.ShapeDtypeStruct((b, l, d), F32), jax.ShapeDtypeStruct((b, n_state), F32),
                   jax.ShapeDtypeStruct((b, n_state), F32)],
        scratch_shapes=[pltpu.VMEM((rows, d), F32), pltpu.VMEM((rows, SSM_LANE_CHUNK), F32),
                        pltpu.VMEM((rows, SSM_LANE_CHUNK), F32), pltpu.VMEM((rows, d), F32),
                        pltpu.VMEM((bb, n_state), F32), pltpu.VMEM((bb, n_state), F32),
                        pltpu.VMEM((rows, rows), BF16), pltpu.VMEM((rows, rows), BF16)],
        compiler_params=_params("arbitrary", "arbitrary"),
        name="s5_mixer",
    )(x2.reshape(b, l, d), nw, mod, mod, mod, b_re_blk, b_im_blk, c_re_blk, c_im_blk, lam_re, lam_im, d_skip,
      w_glu, x0_re, x0_im)
    return out.reshape(t, d), fr, fi


def _s5_discretize(a_re, a_im, log_dt, b_re, b_im, c_re, c_im):
    g, p = a_re.shape
    dt = jnp.exp(log_dt)[:, None]
    mag = jnp.exp(a_re * dt)
    lam_re = mag * jnp.cos(a_im * dt)
    lam_im = mag * jnp.sin(a_im * dt)
    den = a_re * a_re + a_im * a_im
    coef_re = ((lam_re - 1.0) * a_re + lam_im * a_im) / den
    coef_im = (lam_im * a_re - (lam_re - 1.0) * a_im) / den
    bb_re = coef_re[..., None] * b_re - coef_im[..., None] * b_im
    bb_im = coef_re[..., None] * b_im + coef_im[..., None] * b_re
    gpb = SSM_LANE_CHUNK // p
    nblk = g // gpb
    eye = jnp.eye(gpb, dtype=F32)

    def in_map(bb):
        t = bb.reshape(nblk, gpb, p, SSM_GROUP)
        return jnp.einsum("ngpc,gh->ngchp", t, eye).reshape(nblk, gpb * SSM_GROUP, gpb * p).astype(BF16)

    def out_map(c):
        t = c.reshape(nblk, gpb, SSM_GROUP, p)
        return jnp.einsum("ngcp,gh->ngphc", t, eye).reshape(nblk, gpb * p, gpb * SSM_GROUP).astype(BF16)

    return (lam_re.reshape(1, g * p), lam_im.reshape(1, g * p), in_map(bb_re), in_map(bb_im),
            out_map(c_re), out_map(c_im))


def _final_norm_kernel(x_ref, w_ref, o_ref):
    x = x_ref[...]
    ms = jnp.mean(x * x, axis=-1, keepdims=True)
    o_ref[...] = x * lax.rsqrt(ms + NORM_EPS) * w_ref[...]


def _final_norm(x2, w):
    t, d = x2.shape
    rows = min(ROW_TILE, t)
    x_spec = pl.BlockSpec((rows, d), lambda i: (i, 0))
    return pl.pallas_call(
        _final_norm_kernel,
        grid=(t // rows,),
        in_specs=[x_spec, pl.BlockSpec((1, d), lambda i: (0, 0))],
        out_specs=x_spec,
        out_shape=jax.ShapeDtypeStruct((t, d), F32),
        compiler_params=_params("arbitrary"),
        name="final_norm",
    )(x2, w)


def _trunk(xs, dims, mods, s_hgrn, s_re, s_im, p):
    depth = p["norm_w"].shape[0]
    n_groups = len(xs)
    d = xs[0].shape[-1]
    n_hgrn = (depth + 1) // 2
    new_hgrn = [None] * n_groups
    new_re = [[] for _ in xs]
    new_im = [[] for _ in xs]
    for layer in range(depth):
        slot = layer // 2
        nw1 = p["norm_w"][layer, 0].reshape(1, d)
        nw2 = p["norm_w"][layer, 1].reshape(1, d)
        layer_mods = [m[layer] for m in mods]
        if layer % 2 == 0:
            for gi in range(n_groups):
                x2, (b, l), mod = xs[gi], dims[gi], layer_mods[gi]
                q, lf, v, g = _hgrn_in(x2, b, l, mod, nw1, p["hgrn_w_in"], slot, p["log_lb"][slot],
                                       p["log1m_lb"][slot])
                o, new_hgrn[gi] = _gla(q, lf, v, g, b, l, p["hgrn_gnorm"][slot].reshape(1, HEAD_DIM),
                                       s_hgrn[gi], new_hgrn[gi], slot, n_hgrn)
                xs[gi] = _proj_ffn(o, p["hgrn_w_out"], x2, b, l, mod, nw2, p["ffn_w_in"], p["ffn_w_out"], slot)
        else:
            lam_re, lam_im, bre, bim, cre, cim = p["s5"][slot]
            n_state = lam_re.shape[-1]
            for gi in range(n_groups):
                b, l = dims[gi]
                if s_re[gi] is None:
                    x0r = x0i = jnp.zeros((b, n_state), F32)
                else:
                    x0r = s_re[gi][slot].reshape(b, n_state)
                    x0i = s_im[gi][slot].reshape(b, n_state)
                xs[gi], fr, fi = _s5(xs[gi], b, l, layer_mods[gi], nw1, x0r, x0i, lam_re, lam_im, bre, bim, cre,
                                     cim, p["ssm_d"][slot].reshape(1, d), p["ssm_w_glu"], slot)
                new_re[gi].append(fr)
                new_im[gi].append(fi)
            final_w = p["final_norm_w"].reshape(1, d) if layer == depth - 1 else None
            xs = _moe(xs, dims, layer_mods, nw2, p["moe_w_router"][slot], p["moe_b_router"][slot],
                      p["moe_w_in"], p["moe_w_out"], slot, final_w)
    if depth % 2 == 1:
        xs = [_final_norm(x2, p["final_norm_w"].reshape(1, d)) for x2 in xs]
    return xs, new_hgrn, [jnp.stack(r) for r in new_re], [jnp.stack(r) for r in new_im]


def kernel(x_prompt, x_sample, state_hgrn, state_ssm_re, state_ssm_im, c_prompt, c_sample, norm_w, ada_w, ada_b,
           hgrn_w_in, hgrn_lb, hgrn_gnorm, hgrn_w_out, ssm_a_re, ssm_a_im, ssm_log_dt, ssm_b_re, ssm_b_im,
           ssm_c_re, ssm_c_im, ssm_d, ssm_w_glu, ffn_w_in, ffn_w_out, moe_w_router, moe_b_router, moe_w_in,
           moe_w_out, final_norm_w):
    bp, lp, d = x_prompt.shape
    bs, ls, _ = x_sample.shape
    n_ssm, g, n_p = ssm_a_re.shape

    pr = jax.nn.softmax(hgrn_lb, axis=0)
    cs = jnp.cumsum(pr, axis=0)
    lb = cs - cs[0:1]
    p = {
        "norm_w": norm_w,
        "log_lb": jnp.log(lb)[:, None, :],
        "log1m_lb": jnp.log1p(-lb)[:, None, :],
        "hgrn_w_in": hgrn_w_in.astype(BF16),
        "hgrn_gnorm": hgrn_gnorm,
        "hgrn_w_out": hgrn_w_out.astype(BF16),
        "s5": [_s5_discretize(ssm_a_re[i], ssm_a_im[i], ssm_log_dt[i], ssm_b_re[i], ssm_b_im[i],
                              ssm_c_re[i], ssm_c_im[i]) for i in range(n_ssm)],
        "ssm_d": ssm_d,
        "ssm_w_glu": ssm_w_glu.astype(BF16),
        "ffn_w_in": ffn_w_in.astype(BF16),
        "ffn_w_out": ffn_w_out.astype(BF16),
        "moe_w_router": moe_w_router,
        "moe_b_router": moe_b_router,
        "moe_w_in": moe_w_in.astype(BF16),
        "moe_w_out": moe_w_out.astype(BF16),
        "final_norm_w": final_norm_w,
    }

    mods = _modulation(jnp.concatenate([c_prompt, c_sample], axis=0), ada_w, ada_b)
    mods_p = mods[:, :bp]
    mods_s = mods[:, bp:]

    (yp, ys), (hgrn_p, hgrn_s), (re_p, re_s), (im_p, im_s) = _trunk(
        [x_prompt.reshape(bp * lp, d), x_sample.reshape(bs * ls, d)], [(bp, lp), (bs, ls)], [mods_p, mods_s],
        [None, state_hgrn], [None, state_ssm_re], [None, state_ssm_im], p)

    def ssm_shape(a, b):
        return a.reshape(n_ssm, b, g, n_p)

    return (yp.reshape(bp, lp, d), ys.reshape(bs, ls, d),
            hgrn_p, ssm_shape(re_p, bp), ssm_shape(im_p, bp),
            hgrn_s, ssm_shape(re_s, bs), ssm_shape(im_s, bs))
```

```python
import functools
import math

import jax
import jax.numpy as jnp
from jax import lax
from jax.experimental import pallas as pl
from jax.experimental.pallas import tpu as pltpu

F32 = jnp.float32
BF16 = jnp.bfloat16
HIGHEST = lax.Precision.HIGHEST

NORM_EPS = 1e-6
LOG2_E = 1.4426950408889634
LANES = 128
SUBLANES = 8
BF16_ROWS = 16
ROW_TILE = 512
HEAD_DIM = 128
HGRN_IN_COLS = 256
GLA_CHUNK = 64
GLA_HEADS_PER_STEP = 32
SSM_GROUP = 16
SSM_LANE_CHUNK = 1024
MOE_TILE = 768
VMEM_LIMIT = 56 * 1024 * 1024


def _params(*sem):
    return pltpu.CompilerParams(dimension_semantics=sem, vmem_limit_bytes=VMEM_LIMIT)


def _sigmoid(x):
    return 1.0 / (1.0 + jnp.exp(-x))


def _silu(x):
    return x * _sigmoid(x)


def _dot(a, b):
    return jnp.dot(a, b, preferred_element_type=F32)


def _tile_geometry(b, l):
    tl = min(l, ROW_TILE)
    bb = max(1, min(b, ROW_TILE // tl))
    assert l % tl == 0 and b % bb == 0
    return bb, tl


def _mod_rows(m_ref, b0, bb, tl):
    if bb == 1:
        return m_ref[pl.ds(b0, 1), :]
    if bb % SUBLANES == 0:
        b0 = pl.multiple_of(b0, SUBLANES)
    m = m_ref[pl.ds(b0, bb), :]
    return jnp.concatenate([jnp.broadcast_to(m[i : i + 1, :], (tl, m.shape[1])) for i in range(bb)], axis=0)


def _norm_mod(x, nw, sc, sh):
    ms = jnp.mean(x * x, axis=-1, keepdims=True)
    return x * lax.rsqrt(ms + NORM_EPS) * nw * (1.0 + sc) + sh


def _row_specs(b, l, d):
    bb, tl = _tile_geometry(b, l)
    per_batch = l // tl
    x_spec = pl.BlockSpec((bb, tl, d), lambda i, *_: (i // per_batch, i % per_batch, 0))

    def mod_spec(k):
        return pl.BlockSpec((b, d), lambda i, *_: (0, k))

    return x_spec, mod_spec


def _tile(ref):
    bb, tl, d = ref.shape
    return ref[...].reshape(bb * tl, d)


def _mod_kernel(c_ref, w_ref, b_ref, o_ref):
    c = c_ref[...]
    o_ref[0] = _dot(_silu(c).astype(BF16), w_ref[0].astype(BF16)) + b_ref[0]


def _modulation(c_all, ada_w, ada_b):
    depth, d, n = ada_w.shape
    rows = c_all.shape[0]
    tn = 1536
    assert n % tn == 0
    return pl.pallas_call(
        _mod_kernel,
        grid=(depth, n // tn),
        in_specs=[
            pl.BlockSpec((rows, d), lambda l, j: (0, 0)),
            pl.BlockSpec((1, d, tn), lambda l, j: (l, 0, j)),
            pl.BlockSpec((1, 1, tn), lambda l, j: (l, 0, j)),
        ],
        out_specs=pl.BlockSpec((1, rows, tn), lambda l, j: (l, 0, j)),
        out_shape=jax.ShapeDtypeStruct((depth, rows, n), F32),
        compiler_params=_params("arbitrary", "arbitrary"),
        name="adaln_modulation",
    )(c_all, ada_w, ada_b.reshape(depth, 1, n))


def _hgrn_in_kernel(x_ref, nw_ref, sh_ref, sc_ref, w_ref, la_ref, lc_ref, q_ref, lf_ref, v_ref, g_ref,
                    *, bb, tl, l_total):
    d = x_ref.shape[-1]
    rows = bb * tl
    b0 = (pl.program_id(0) * rows) // l_total
    h = _norm_mod(_tile(x_ref), nw_ref[...], _mod_rows(sc_ref, b0, bb, tl), _mod_rows(sh_ref, b0, bb, tl))
    h = h.astype(BF16)

    def put(ref, cs, val):
        ref[:, :, cs] = val.astype(ref.dtype).reshape(bb, tl, HGRN_IN_COLS)

    for c0 in range(0, d, HGRN_IN_COLS):
        cs = slice(c0, c0 + HGRN_IN_COLS)
        put(q_ref, cs, _silu(_dot(h, w_ref[:, c0 : c0 + HGRN_IN_COLS])))
        f = _dot(h, w_ref[:, d + c0 : d + c0 + HGRN_IN_COLS])
        log_sig = jnp.minimum(f, 0.0) - jnp.log1p(jnp.exp(-jnp.abs(f)))
        c = lc_ref[:, cs] + log_sig
        a = la_ref[:, cs]
        put(lf_ref, cs, jnp.maximum(a, c) + jnp.log1p(jnp.exp(-jnp.abs(a - c))))
        put(v_ref, cs, _dot(h, w_ref[:, 2 * d + c0 : 2 * d + c0 + HGRN_IN_COLS]))
        put(g_ref, cs, _silu(_dot(h, w_ref[:, 3 * d + c0 : 3 * d + c0 + HGRN_IN_COLS])))


def _hgrn_in(x3, mod, nw, w_in, slot, log_lb, log1m_lb):
    b, l, d = x3.shape
    bb, tl = _tile_geometry(b, l)
    x_spec, mod_spec = _row_specs(b, l, d)
    row = pl.BlockSpec((1, d), lambda i: (0, 0))
    act = BF16 if l % BF16_ROWS == 0 else F32
    return pl.pallas_call(
        functools.partial(_hgrn_in_kernel, bb=bb, tl=tl, l_total=l),
        grid=(b * l // (bb * tl),),
        in_specs=[x_spec, row, mod_spec(0), mod_spec(1),
                  pl.BlockSpec((None, d, 4 * d), lambda i: (slot, 0, 0)), row, row],
        out_specs=[x_spec, x_spec, x_spec, x_spec],
        out_shape=[jax.ShapeDtypeStruct((b, l, d), act), jax.ShapeDtypeStruct((b, l, d), F32),
                   jax.ShapeDtypeStruct((b, l, d), act), jax.ShapeDtypeStruct((b, l, d), act)],
        compiler_params=_params("arbitrary"),
        name="hgrn_in_proj",
    )(x3, nw, mod, mod, w_in, log_lb, log1m_lb)


def _cumsum_rows(x):
    n = x.shape[0]
    row = lax.broadcasted_iota(jnp.int32, x.shape, 0)
    shift = 1
    while shift < n:
        x = x + jnp.where(row >= shift, pltpu.roll(x, shift, axis=0), 0.0)
        shift *= 2
    return x


def _gla_head(q, lf, v, g, gw, s, c):
    nb = c // SUBLANES
    lf2 = lf * LOG2_E
    b = _cumsum_rows(lf2)
    k = 1.0 - jnp.exp2(lf2)
    vb = v.astype(BF16)

    o = _dot((q * jnp.exp2(b)).astype(BF16), s.astype(BF16))

    b3 = b.reshape(nb, SUBLANES, HEAD_DIM)
    q3 = q.reshape(nb, SUBLANES, HEAD_DIM)
    k3 = k.reshape(nb, SUBLANES, HEAD_DIM)
    v3 = v.reshape(nb, SUBLANES, HEAD_DIM)
    t_in = lax.broadcasted_iota(jnp.int32, (nb, SUBLANES, HEAD_DIM), 1)
    o_diag = jnp.zeros((nb, SUBLANES, HEAD_DIM), F32)
    for s_in in range(SUBLANES):
        diff = b3 - b3[:, s_in : s_in + 1, :]
        e = jnp.exp2(jnp.where(t_in >= s_in, diff, -jnp.inf))
        w = jnp.sum(q3 * e * k3[:, s_in : s_in + 1, :], axis=-1, keepdims=True)
        o_diag = o_diag + w * v3[:, s_in : s_in + 1, :]
    o = o + o_diag.reshape(c, HEAD_DIM)

    if c > SUBLANES:
        row = lax.broadcasted_iota(jnp.int32, (c, c), 0)
        col = lax.broadcasted_iota(jnp.int32, (c, c), 1)
        t_row = lax.broadcasted_iota(jnp.int32, (c, HEAD_DIM), 0)
        a = jnp.zeros((c, c), F32)
        half = SUBLANES
        while half < c:
            span = 2 * half
            m = b.reshape(c // span, span, HEAD_DIM)[:, half - 1 : half, :]
            m = jnp.broadcast_to(m, (c // span, span, HEAD_DIM)).reshape(c, HEAD_DIM)
            right = (t_row & half) != 0
            qh = q * jnp.exp2(jnp.where(right, b - m, -jnp.inf))
            kh = k * jnp.exp2(jnp.where(right, -jnp.inf, m - b))
            sc = lax.dot_general(qh.astype(BF16), kh.astype(BF16), (((1,), (1,)), ((), ())),
                                 preferred_element_type=F32)
            shift = int(math.log2(span))
            a = a + jnp.where((row >> shift) == (col >> shift), sc, 0.0)
            half = span
        o = o + _dot(a.astype(BF16), vb)

    b_last = b[c - 1 : c, :]
    kd = k * jnp.exp2(b_last - b)
    pad = HEAD_DIM - c
    x = jnp.concatenate([kd, jnp.broadcast_to(jnp.exp2(b_last), (pad, HEAD_DIM))], axis=0)
    xt = x.T
    v_pad = jnp.concatenate([v, jnp.zeros((pad, HEAD_DIM), F32)], axis=0).astype(BF16)
    s_new = xt[:, c : c + 1] * s + _dot(xt.astype(BF16), v_pad)

    ms = jnp.mean(o * o, axis=-1, keepdims=True)
    return o * lax.rsqrt(ms + NORM_EPS) * gw * g, s_new


def _gla_kernel(*refs, c, heads, n_chunks, has_s0):
    q_ref, lf_ref, v_ref, g_ref, gw_ref = refs[:5]
    s0_ref = refs[5] if has_s0 else None
    o_ref, sf_ref, s_scr = refs[-3:]
    step = pl.program_id(1)

    @pl.when(step == 0)
    def _():
        if has_s0:
            s_scr[...] = s0_ref[...]
        else:
            s_scr[...] = jnp.zeros_like(s_scr)

    gw = gw_ref[...]
    for bi in range(q_ref.shape[0]):
        for h in range(heads):
            sl = slice(h * HEAD_DIM, (h + 1) * HEAD_DIM)
            out, s_new = _gla_head(q_ref[bi, :, sl].astype(F32), lf_ref[bi, :, sl], v_ref[bi, :, sl].astype(F32),
                                   g_ref[bi, :, sl].astype(F32), gw, s_scr[bi, h], c)
            o_ref[bi, :, sl] = out.astype(o_ref.dtype)
            s_scr[bi, h] = s_new

    @pl.when(step == n_chunks - 1)
    def _():
        sf_ref[...] = s_scr[...]


def _gla(q3, lf3, v3, g3, gnorm_w, s0_all, s_prev, slot, n_slots):
    b, l, d = q3.shape
    heads = d // HEAD_DIM
    c = math.gcd(l, GLA_CHUNK)
    n_chunks = l // c
    bt = max(1, min(b, GLA_HEADS_PER_STEP // heads // max(1, c // 32)))
    assert b % bt == 0
    has_s0 = s0_all is not None
    blk = pl.BlockSpec((bt, c, d), lambda i, j: (i, j, 0))
    s_spec = pl.BlockSpec((None, bt, heads, HEAD_DIM, HEAD_DIM), lambda i, j: (slot, i, 0, 0, 0))
    args = [q3, lf3, v3, g3, gnorm_w]
    in_specs = [blk, blk, blk, blk, pl.BlockSpec((1, HEAD_DIM), lambda i, j: (0, 0))]
    if has_s0:
        args.append(s0_all)
        in_specs.append(s_spec)
    if s_prev is not None:
        args.append(s_prev)
        in_specs.append(pl.BlockSpec(memory_space=pl.ANY))
    o, s_fin = pl.pallas_call(
        functools.partial(_gla_kernel, c=c, heads=heads, n_chunks=n_chunks, has_s0=has_s0),
        grid=(b // bt, n_chunks),
        in_specs=in_specs,
        out_specs=[blk, s_spec],
        out_shape=[jax.ShapeDtypeStruct((b, l, d), q3.dtype),
                   jax.ShapeDtypeStruct((n_slots, b, heads, HEAD_DIM, HEAD_DIM), F32)],
        input_output_aliases={len(args) - 1: 1} if s_prev is not None else {},
        scratch_shapes=[pltpu.VMEM((bt, heads, HEAD_DIM, HEAD_DIM), F32)],
        compiler_params=_params("arbitrary", "arbitrary"),
        name="hgrn_gla_scan",
    )(*args)
    return o, s_fin


def _ff_tile(d_ff):
    best = LANES
    for t in range(LANES, 1408 + 1, LANES):
        if d_ff % t == 0:
            best = t
    return best


def _ffn_kernel(a_ref, wp_ref, g1_ref, x_ref, nw_ref, sh_ref, sc_ref, gt_ref, wg_ref, wu_ref, wo_ref, o_ref,
                x1_scr, h_scr, acc_scr, *, nf, bb, tl, l_total):
    rows = bb * tl
    b0 = (pl.program_id(0) * rows) // l_total
    j = pl.program_id(1)

    @pl.when(j == 0)
    def _():
        x1 = _tile(x_ref) + _mod_rows(g1_ref, b0, bb, tl) * _dot(_tile(a_ref).astype(BF16), wp_ref[...])
        x1_scr[...] = x1
        h = _norm_mod(x1, nw_ref[...], _mod_rows(sc_ref, b0, bb, tl), _mod_rows(sh_ref, b0, bb, tl))
        h_scr[...] = h.astype(BF16)
        acc_scr[...] = jnp.zeros_like(acc_scr)

    h = h_scr[...]
    act = (_silu(_dot(h, wg_ref[...])) * _dot(h, wu_ref[...])).astype(BF16)
    acc_scr[...] += _dot(act, wo_ref[...])

    @pl.when(j == nf - 1)
    def _():
        o_ref[...] = (x1_scr[...] + _mod_rows(gt_ref, b0, bb, tl) * acc_scr[...]).reshape(o_ref.shape)


def _proj_ffn(a3, w_proj, x3, mod, nw, w_in, w_out, slot):
    b, l, d = x3.shape
    k = a3.shape[-1]
    assert k == d
    d_ff = w_out.shape[-2]
    tf = _ff_tile(d_ff)
    nf = d_ff // tf
    bb, tl = _tile_geometry(b, l)
    rows = bb * tl
    x_spec, mod_spec = _row_specs(b, l, d)
    return pl.pallas_call(
        functools.partial(_ffn_kernel, nf=nf, bb=bb, tl=tl, l_total=l),
        grid=(b * l // rows, nf),
        in_specs=[
            x_spec, pl.BlockSpec((None, k, d), lambda i, j: (slot, 0, 0)),
            mod_spec(2),
            x_spec, pl.BlockSpec((1, d), lambda i, j: (0, 0)), mod_spec(3), mod_spec(4), mod_spec(5),
            pl.BlockSpec((None, d, tf), lambda i, j: (slot, 0, j)),
            pl.BlockSpec((None, d, tf), lambda i, j: (slot, 0, nf + j)),
            pl.BlockSpec((None, tf, d), lambda i, j: (slot, j, 0)),
        ],
        out_specs=x_spec,
        out_shape=jax.ShapeDtypeStruct((b, l, d), F32),
        scratch_shapes=[pltpu.VMEM((rows, d), F32), pltpu.VMEM((rows, d), BF16), pltpu.VMEM((rows, d), F32)],
        compiler_params=_params("arbitrary", "arbitrary"),
        name="proj_dense_swiglu",
    )(a3, w_proj, mod, x3, nw, mod, mod, mod, w_in, w_in, w_out)


def _route_kernel(x_ref, nw_ref, sh_ref, sc_ref, wr_ref, br_ref, cnt0_ref, h_ref, pos_ref, prob_ref, cnt_ref,
                  cnt_scr, *, cap, n_steps, bb, tl, l_total):
    rows = bb * tl
    step = pl.program_id(0)
    b0 = (step * rows) // l_total

    @pl.when(step == 0)
    def _():
        cnt_scr[...] = cnt0_ref[...]

    h = _norm_mod(_tile(x_ref), nw_ref[...], _mod_rows(sc_ref, b0, bb, tl), _mod_rows(sh_ref, b0, bb, tl))
    h_ref[...] = h
    logits = jnp.dot(h, wr_ref[...], preferred_element_type=F32, precision=HIGHEST) + br_ref[...]
    lane = lax.broadcasted_iota(jnp.int32, (rows, LANES), 1)
    m0 = jnp.max(logits, axis=-1, keepdims=True)
    i0 = jnp.min(jnp.where(logits == m0, lane, LANES), axis=-1, keepdims=True)
    rest = jnp.where(lane == i0, -jnp.inf, logits)
    m1 = jnp.max(rest, axis=-1, keepdims=True)
    i1 = jnp.min(jnp.where(rest == m1, lane, LANES), axis=-1, keepdims=True)
    e1 = jnp.exp(m1 - m0)
    p0 = 1.0 / (1.0 + e1)
    p1 = e1 / (1.0 + e1)
    hit0 = lane == i0
    hit1 = lane == i1
    onehot = jnp.where(hit0 | hit1, 1.0, 0.0)
    r_i = lax.broadcasted_iota(jnp.int32, (rows, rows), 0)
    c_i = lax.broadcasted_iota(jnp.int32, (rows, rows), 1)
    before = _dot((r_i > c_i).astype(BF16), onehot.astype(BF16)) + cnt_scr[...]
    rank0 = jnp.sum(jnp.where(hit0, before, 0.0), axis=-1, keepdims=True).astype(jnp.int32)
    rank1 = jnp.sum(jnp.where(hit1, before, 0.0), axis=-1, keepdims=True).astype(jnp.int32)
    pos_ref[...] = jnp.where(lane == 0, i0 * cap + rank0, jnp.where(lane == 1, i1 * cap + rank1, 0))
    prob_ref[...] = jnp.where(lane == 0, p0, jnp.where(lane == 1, p1, 0.0))
    cnt_scr[...] += jnp.sum(onehot, axis=0, keepdims=True)

    @pl.when(step == n_steps - 1)
    def _():
        cnt_ref[...] = cnt_scr[...]


def _route(x3, mod, nw, wr, br, cnt0, cap):
    b, l, d = x3.shape
    t = b * l
    bb, tl = _tile_geometry(b, l)
    rows = bb * tl
    n_steps = t // rows
    x_spec, mod_spec = _row_specs(b, l, d)
    row = pl.BlockSpec((1, LANES), lambda i: (0, 0))
    lanes_spec = pl.BlockSpec((rows, LANES), lambda i: (i, 0))
    return pl.pallas_call(
        functools.partial(_route_kernel, cap=cap, n_steps=n_steps, bb=bb, tl=tl, l_total=l),
        grid=(n_steps,),
        in_specs=[x_spec, pl.BlockSpec((1, d), lambda i: (0, 0)), mod_spec(3), mod_spec(4),
                  pl.BlockSpec((d, LANES), lambda i: (0, 0)), row, row],
        out_specs=[pl.BlockSpec((rows, d), lambda i: (i, 0)), lanes_spec, lanes_spec, row],
        out_shape=[jax.ShapeDtypeStruct((t, d), F32), jax.ShapeDtypeStruct((t, LANES), jnp.int32),
                   jax.ShapeDtypeStruct((t, LANES), F32), jax.ShapeDtypeStruct((1, LANES), F32)],
        scratch_shapes=[pltpu.VMEM((1, LANES), F32)],
        compiler_params=_params("arbitrary"),
        name="moe_route",
    )(x3, nw, mod, mod, wr, br, cnt0)


def _row_copy(src, src_row, dst, dst_row, sem):
    return pltpu.make_async_copy(src.at[pl.ds(src_row, 1)], dst.at[pl.ds(dst_row, 1)], sem)


def _scatter_kernel(cnt_ref, pos_ref, h_ref, *rest, rows, cap, n_steps, fill):
    if fill:
        _, xg_ref, zero_blk, sem = rest
    else:
        xg_ref, zero_blk, sem = rest

    for r in range(rows):
        _row_copy(h_ref, r, xg_ref, pos_ref[0, 0, 2 * r], sem).start(priority=0)
        _row_copy(h_ref, r, xg_ref, pos_ref[0, 0, 2 * r + 1], sem).start(priority=1)
    for _ in range(2):
        pltpu.make_async_copy(h_ref, xg_ref.at[pl.ds(0, rows)], sem).wait()

    if fill:
        @pl.when(pl.program_id(0) == n_steps - 1)
        def _():
            zero_blk[...] = jnp.zeros_like(zero_blk)
            for e in range(cnt_ref.shape[0]):
                cnt = cnt_ref[e]
                n_pad = lax.rem(MOE_TILE - lax.rem(cnt, MOE_TILE), MOE_TILE)
                n_single = lax.rem(SUBLANES - lax.rem(cnt, SUBLANES), SUBLANES)
                n_blocks = (n_pad - n_single) // SUBLANES
                base = e * cap + cnt

                def single(r, wait, base=base):
                    cp = _row_copy(zero_blk, 0, xg_ref, base + r, sem)
                    cp.wait() if wait else cp.start()

                def block(k, wait, base=base + n_single):
                    dst = xg_ref.at[pl.ds(pl.multiple_of(base + k * SUBLANES, SUBLANES), SUBLANES)]
                    cp = pltpu.make_async_copy(zero_blk, dst, sem)
                    cp.wait() if wait else cp.start()

                for wait in (False, True):
                    lax.fori_loop(0, n_single, lambda r, c, w=wait: single(r, w) or c, 0)
                    lax.fori_loop(0, n_blocks, lambda k, c, w=wait: block(k, w) or c, 0)


def _scatter(counts, pos, h2, xg, n_slots, cap):
    t, d = h2.shape
    rows = min(ROW_TILE, t)
    n_steps = t // rows
    fill = xg is not None
    in_specs = [pl.BlockSpec((1, 1, 2 * rows), lambda i, c: (i, 0, 0), memory_space=pltpu.SMEM),
                pl.BlockSpec((rows, d), lambda i, c: (i, 0))]
    args = [counts, pos, h2]
    if fill:
        in_specs.append(pl.BlockSpec(memory_space=pl.ANY))
        args.append(xg)
    return pl.pallas_call(
        functools.partial(_scatter_kernel, rows=rows, cap=cap, n_steps=n_steps, fill=fill),
        grid_spec=pltpu.PrefetchScalarGridSpec(
            num_scalar_prefetch=1, grid=(n_steps,), in_specs=in_specs,
            out_specs=pl.BlockSpec(memory_space=pl.ANY),
            scratch_shapes=[pltpu.VMEM((SUBLANES, d), F32), pltpu.SemaphoreType.DMA(())]),
        out_shape=jax.ShapeDtypeStruct((n_slots, d), F32),
        input_output_aliases={3: 0} if fill else {},
        compiler_params=_params("arbitrary"),
        name="moe_scatter",
    )(*args)


def _gmm_kernel(blk_ref, exp_ref, used_ref, x_ref, wg_ref, wu_ref, wo_ref, o_ref, xb_scr):
    i = pl.program_id(0)
    j = pl.program_id(1)

    @pl.when(i < used_ref[0])
    def _():
        @pl.when(j == 0)
        def _():
            xb_scr[...] = x_ref[...].astype(BF16)

        h = xb_scr[...]
        act = (_silu(_dot(h, wg_ref[...])) * _dot(h, wu_ref[...])).astype(BF16)
        y = _dot(act, wo_ref[...])

        @pl.when(j == 0)
        def _():
            o_ref[...] = y

        @pl.when(j > 0)
        def _():
            o_ref[...] += y


def _gmm(tile_blk, tile_exp, n_used, xg, w_in, w_out, slot):
    n_slots, d = xg.shape
    d_ff = w_out.shape[-2]
    tf = _ff_tile(d_ff)
    nf = d_ff // tf
    n_tiles = tile_blk.shape[0]

    def ff(i, j, used):
        return jnp.where(i < used[0], j, nf - 1)

    row_spec = pl.BlockSpec((MOE_TILE, d), lambda i, j, blk, ex, used: (blk[i], 0))
    return pl.pallas_call(
        _gmm_kernel,
        grid_spec=pltpu.PrefetchScalarGridSpec(
            num_scalar_prefetch=3, grid=(n_tiles, nf),
            in_specs=[
                row_spec,
                pl.BlockSpec((None, None, d, tf), lambda i, j, blk, ex, used: (slot, ex[i], 0, ff(i, j, used))),
                pl.BlockSpec((None, None, d, tf),
                             lambda i, j, blk, ex, used: (slot, ex[i], 0, nf + ff(i, j, used))),
                pl.BlockSpec((None, None, tf, d), lambda i, j, blk, ex, used: (slot, ex[i], ff(i, j, used), 0)),
            ],
            out_specs=row_spec,
            scratch_shapes=[pltpu.VMEM((MOE_TILE, d), BF16)]),
        out_shape=jax.ShapeDtypeStruct((n_slots, d), F32),
        compiler_params=_params("arbitrary", "arbitrary"),
        name="moe_grouped_swiglu",
    )(tile_blk, tile_exp, n_used, xg, w_in, w_in, w_out)


def _combine_kernel(pos_ref, x_ref, gt_ref, prob_ref, y_ref, *rest, bb, tl, l_total, final):
    fw_ref = rest[0] if final else None
    o_ref, buf0, buf1, sem = rest[-4:]
    rows = bb * tl
    b0 = (pl.program_id(0) * rows) // l_total
    for r in range(rows):
        _row_copy(y_ref, pos_ref[0, 0, 2 * r], buf0, r, sem).start(priority=0)
        _row_copy(y_ref, pos_ref[0, 0, 2 * r + 1], buf1, r, sem).start(priority=1)
    for buf in (buf0, buf1):
        pltpu.make_async_copy(y_ref.at[pl.ds(0, rows)], buf, sem).wait()
    prob = prob_ref[...]
    f = prob[:, 0:1] * buf0[...] + prob[:, 1:2] * buf1[...]
    x = _tile(x_ref) + _mod_rows(gt_ref, b0, bb, tl) * f
    if final:
        ms = jnp.mean(x * x, axis=-1, keepdims=True)
        x = x * lax.rsqrt(ms + NORM_EPS) * fw_ref[...]
    o_ref[...] = x.reshape(o_ref.shape)


def _combine(pos, x3, mod, prob, y, final_w):
    b, l, d = x3.shape
    bb, tl = _tile_geometry(b, l)
    rows = bb * tl
    x_spec, mod_spec = _row_specs(b, l, d)
    final = final_w is not None
    in_specs = [pl.BlockSpec((1, 1, 2 * rows), lambda i: (i, 0, 0), memory_space=pltpu.SMEM),
                x_spec, mod_spec(5), pl.BlockSpec((rows, LANES), lambda i: (i, 0)),
                pl.BlockSpec(memory_space=pl.ANY)]
    args = [pos, x3, mod, prob, y]
    if final:
        in_specs.append(pl.BlockSpec((1, d), lambda i: (0, 0)))
        args.append(final_w)
    return pl.pallas_call(
        functools.partial(_combine_kernel, bb=bb, tl=tl, l_total=l, final=final),
        grid=(b * l // rows,),
        in_specs=in_specs,
        out_specs=x_spec,
        out_shape=jax.ShapeDtypeStruct((b, l, d), F32),
        scratch_shapes=[pltpu.VMEM((rows, d), F32), pltpu.VMEM((rows, d), F32), pltpu.SemaphoreType.DMA(())],
        compiler_params=_params("arbitrary"),
        name="moe_combine",
    )(*args)


def _moe(xs, mods, nw, w_router, b_router, w_in, w_out, slot, final_w=None):
    d = xs[0].shape[-1]
    n_exp = w_in.shape[1]
    t_total = sum(x.shape[0] * x.shape[1] for x in xs)
    cap = -(-t_total // MOE_TILE) * MOE_TILE
    n_slots = n_exp * cap
    n_tiles = -(-(2 * t_total + n_exp * (MOE_TILE - 1)) // MOE_TILE)
    wr = jnp.zeros((d, LANES), F32).at[:, :n_exp].set(w_router)
    br = jnp.full((1, LANES), -jnp.inf, F32).at[0, :n_exp].set(b_router)

    cnt = jnp.zeros((1, LANES), F32)
    routed = []
    for x3, mod in zip(xs, mods):
        h2, pos, prob, cnt = _route(x3, mod, nw, wr, br, cnt, cap)
        rows = min(ROW_TILE, pos.shape[0])
        routed.append((h2, pos[:, :2].reshape(pos.shape[0] // rows, 1, 2 * rows), prob))
    counts = cnt[0, :n_exp].astype(jnp.int32)

    xg = None
    for h2, pos, _ in routed:
        xg = _scatter(counts, pos, h2, xg, n_slots, cap)

    tiles = (counts + (MOE_TILE - 1)) // MOE_TILE
    ends = jnp.cumsum(tiles)
    n_used = ends[-1:]
    idx = jnp.minimum(jnp.arange(n_tiles, dtype=jnp.int32), n_used - 1)
    tile_exp = jnp.sum((idx[:, None] >= ends[None, :]).astype(jnp.int32), axis=1)
    tile_blk = tile_exp * (cap // MOE_TILE) + idx - (ends - tiles)[tile_exp]
    y = _gmm(tile_blk.astype(jnp.int32), tile_exp, n_used.astype(jnp.int32), xg, w_in, w_out, slot)

    return [_combine(pos, x3, mod, prob, y, final_w) for x3, mod, (_, pos, prob) in zip(xs, mods, routed)]


def _gelu_tanh(x):
    return 0.5 * x * (1.0 + jnp.tanh(math.sqrt(2.0 / math.pi) * (x + 0.044715 * (x * x * x))))


def _s5_kernel(x_ref, nw_ref, sh_ref, sc_ref, gt_ref, bre_ref, bim_ref, cre_ref, cim_ref, lr_ref, li_ref,
               dsk_ref, wglu_ref, x0r_ref, x0i_ref, o_ref, fr_ref, fi_ref,
               u_scr, bur, bui, y_scr, sr, si, to_tm, to_bm, *, n_steps):
    bb, tc, d = x_ref.shape
    rows = bb * tc
    n_lane_chunks = bre_ref.shape[0]
    in_chunk = bre_ref.shape[1]
    half_lanes = SSM_LANE_CHUNK // 2
    b0 = pl.program_id(0) * bb
    t = pl.program_id(1)

    @pl.when(t == 0)
    def _():
        sr[...] = x0r_ref[...]
        si[...] = x0i_ref[...]
        r_i = lax.broadcasted_iota(jnp.int32, (rows, rows), 0)
        c_i = lax.broadcasted_iota(jnp.int32, (rows, rows), 1)
        bb_bits, tc_bits = bb.bit_length() - 1, tc.bit_length() - 1
        to_tm[...] = (c_i == ((r_i & (bb - 1)) << tc_bits) + (r_i >> bb_bits)).astype(BF16)
        to_bm[...] = (c_i == ((r_i & (tc - 1)) << bb_bits) + (r_i >> tc_bits)).astype(BF16)

    x = x_ref[...].reshape(rows, d)
    h = _norm_mod(x, nw_ref[...], _mod_rows(sc_ref, b0, bb, tc), _mod_rows(sh_ref, b0, bb, tc))
    h_hi = h.astype(BF16)
    h_lo = (h - h_hi.astype(F32)).astype(BF16)
    u_scr[...] = _dot(to_tm[...], h_hi) + _dot(to_tm[...], h_lo)

    for n in range(n_lane_chunks):
        u = u_scr[:, n * in_chunk : (n + 1) * in_chunk].astype(BF16)
        bur[...] = _dot(u, bre_ref[n])
        bui[...] = _dot(u, bim_ref[n])
        for hf in range(2):
            ls = slice(hf * half_lanes, (hf + 1) * half_lanes)
            g0 = n * SSM_LANE_CHUNK + hf * half_lanes
            lam_r = jnp.broadcast_to(lr_ref[:, g0 : g0 + half_lanes], (SUBLANES, half_lanes))
            lam_i = jnp.broadcast_to(li_ref[:, g0 : g0 + half_lanes], (SUBLANES, half_lanes))

            def batch_tile(bt, _):
                r0 = pl.multiple_of(bt * SUBLANES, SUBLANES)

                def step(l, carry):
                    xr, xi = carry
                    row = pl.multiple_of(l * bb + r0, SUBLANES)
                    nr = lam_r * xr - lam_i * xi + bur[pl.ds(row, SUBLANES), ls]
                    ni = lam_r * xi + lam_i * xr + bui[pl.ds(row, SUBLANES), ls]
                    bur[pl.ds(row, SUBLANES), ls] = nr
                    bui[pl.ds(row, SUBLANES), ls] = ni
                    return nr, ni

                xr, xi = lax.fori_loop(
                    0, tc, step,
                    (sr[pl.ds(r0, SUBLANES), g0 : g0 + half_lanes], si[pl.ds(r0, SUBLANES), g0 : g0 + half_lanes]),
                    unroll=8)
                sr[pl.ds(r0, SUBLANES), g0 : g0 + half_lanes] = xr
                si[pl.ds(r0, SUBLANES), g0 : g0 + half_lanes] = xi
                return 0

            lax.fori_loop(0, bb // SUBLANES, batch_tile, 0)
        y_scr[:, n * in_chunk : (n + 1) * in_chunk] = (
            _dot(bur[...].astype(BF16), cre_ref[n]) - _dot(bui[...].astype(BF16), cim_ref[n]))

    y = y_scr[...] + dsk_ref[...] * u_scr[...]
    z = _dot(to_bm[...], _gelu_tanh(y).astype(BF16)).astype(BF16)
    zz = _dot(z, wglu_ref[...])
    m = zz[:, :d] * _sigmoid(zz[:, d:])
    o_ref[...] = (x + _mod_rows(gt_ref, b0, bb, tc) * m).reshape(bb, tc, d)

    @pl.when(t == n_steps - 1)
    def _():
        fr_ref[...] = sr[...]
        fi_ref[...] = si[...]


def _s5(x3, mod, nw, x0_re, x0_im, lam_re, lam_im, b_re_blk, b_im_blk, c_re_blk, c_im_blk, d_skip, w_glu, slot):
    b, l, d = x3.shape
    n_state = lam_re.shape[-1]
    if b * l <= ROW_TILE:
        bb, tc = b, l
    elif b <= ROW_TILE // SUBLANES:
        bb, tc = b, ROW_TILE // b
    else:
        tc = l
        bb = ROW_TILE // l
    assert l % tc == 0 and b % bb == 0 and bb % SUBLANES == 0 and tc % SUBLANES == 0
    assert bb & (bb - 1) == 0 and tc & (tc - 1) == 0
    n_steps = l // tc
    rows = tc * bb
    x_spec = pl.BlockSpec((bb, tc, d), lambda i, s: (i, s, 0))

    def mod_spec(k):
        return pl.BlockSpec((b, d), lambda i, s: (0, k))

    def full(a):
        nd = a.ndim
        return pl.BlockSpec(a.shape, lambda i, s: (0,) * nd)

    st_spec = pl.BlockSpec((bb, n_state), lambda i, s: (i, 0))
    out, fr, fi = pl.pallas_call(
        functools.partial(_s5_kernel, n_steps=n_steps),
        grid=(b // bb, n_steps),
        in_specs=[x_spec, full(nw), mod_spec(0), mod_spec(1), mod_spec(2), full(b_re_blk), full(b_im_blk),
                  full(c_re_blk), full(c_im_blk), full(lam_re), full(lam_im), full(d_skip),
                  pl.BlockSpec((None,) + w_glu.shape[1:], lambda i, s: (slot, 0, 0)),
                  st_spec, st_spec],
        out_specs=[x_spec, st_spec, st_spec],
        out_shape=[jax.ShapeDtypeStruct((b, l, d), F32), jax.ShapeDtypeStruct((b, n_state), F32),
                   jax.ShapeDtypeStruct((b, n_state), F32)],
        scratch_shapes=[pltpu.VMEM((rows, d), F32), pltpu.VMEM((rows, SSM_LANE_CHUNK), F32),
                        pltpu.VMEM((rows, SSM_LANE_CHUNK), F32), pltpu.VMEM((rows, d), F32),
                        pltpu.VMEM((bb, n_state), F32), pltpu.VMEM((bb, n_state), F32),
                        pltpu.VMEM((rows, rows), BF16), pltpu.VMEM((rows, rows), BF16)],
        compiler_params=_params("arbitrary", "arbitrary"),
        name="s5_mixer",
    )(x3, nw, mod, mod, mod, b_re_blk, b_im_blk, c_re_blk, c_im_blk, lam_re, lam_im, d_skip, w_glu, x0_re, x0_im)
    return out, fr, fi


def _s5_discretize(a_re, a_im, log_dt, b_re, b_im, c_re, c_im):
    g, p = a_re.shape
    dt = jnp.exp(log_dt)[:, None]
    mag = jnp.exp(a_re * dt)
    lam_re = mag * jnp.cos(a_im * dt)
    lam_im = mag * jnp.sin(a_im * dt)
    den = a_re * a_re + a_im * a_im
    coef_re = ((lam_re - 1.0) * a_re + lam_im * a_im) / den
    coef_im = (lam_im * a_re - (lam_re - 1.0) * a_im) / den
    bb_re = coef_re[..., None] * b_re - coef_im[..., None] * b_im
    bb_im = coef_re[..., None] * b_im + coef_im[..., None] * b_re
    gpb = SSM_LANE_CHUNK // p
    nblk = g // gpb
    eye = jnp.eye(gpb, dtype=F32)

    def in_map(bb):
        t = bb.reshape(nblk, gpb, p, SSM_GROUP)
        return jnp.einsum("ngpc,gh->ngchp", t, eye).reshape(nblk, gpb * SSM_GROUP, gpb * p).astype(BF16)

    def out_map(c):
        t = c.reshape(nblk, gpb, SSM_GROUP, p)
        return jnp.einsum("ngcp,gh->ngphc", t, eye).reshape(nblk, gpb * p, gpb * SSM_GROUP).astype(BF16)

    return (lam_re.reshape(1, g * p), lam_im.reshape(1, g * p), in_map(bb_re), in_map(bb_im),
            out_map(c_re), out_map(c_im))


def _final_norm_kernel(x_ref, w_ref, o_ref):
    x = x_ref[...]
    ms = jnp.mean(x * x, axis=-1, keepdims=True)
    o_ref[...] = x * lax.rsqrt(ms + NORM_EPS) * w_ref[...]


def _final_norm(x3, w):
    b, l, d = x3.shape
    bb, tl = _tile_geometry(b, l)
    x_spec, _ = _row_specs(b, l, d)
    return pl.pallas_call(
        _final_norm_kernel,
        grid=(b * l // (bb * tl),),
        in_specs=[x_spec, pl.BlockSpec((1, d), lambda i: (0, 0))],
        out_specs=x_spec,
        out_shape=jax.ShapeDtypeStruct((b, l, d), F32),
        compiler_params=_params("arbitrary"),
        name="final_norm",
    )(x3, w)


def _trunk(xs, mods, s_hgrn, s_re, s_im, p):
    depth = p["norm_w"].shape[0]
    n_groups = len(xs)
    d = xs[0].shape[-1]
    n_hgrn = (depth + 1) // 2
    new_hgrn = [None] * n_groups
    new_re = [[] for _ in xs]
    new_im = [[] for _ in xs]
    for layer in range(depth):
        slot = layer // 2
        nw1 = p["norm_w"][layer, 0].reshape(1, d)
        nw2 = p["norm_w"][layer, 1].reshape(1, d)
        layer_mods = [m[layer] for m in mods]
        if layer % 2 == 0:
            for gi in range(n_groups):
                x3, mod = xs[gi], layer_mods[gi]
                q, lf, v, g = _hgrn_in(x3, mod, nw1, p["hgrn_w_in"], slot, p["log_lb"][slot], p["log1m_lb"][slot])
                o, new_hgrn[gi] = _gla(q, lf, v, g, p["hgrn_gnorm"][slot].reshape(1, HEAD_DIM),
                                       s_hgrn[gi], new_hgrn[gi], slot, n_hgrn)
                xs[gi] = _proj_ffn(o, p["hgrn_w_out"], x3, mod, nw2, p["ffn_w_in"], p["ffn_w_out"], slot)
        else:
            lam_re, lam_im, bre, bim, cre, cim = p["s5"][slot]
            n_state = lam_re.shape[-1]
            for gi in range(n_groups):
                b = xs[gi].shape[0]
                if s_re[gi] is None:
                    x0r = x0i = jnp.zeros((b, n_state), F32)
                else:
                    x0r = s_re[gi][slot].reshape(b, n_state)
                    x0i = s_im[gi][slot].reshape(b, n_state)
                xs[gi], fr, fi = _s5(xs[gi], layer_mods[gi], nw1, x0r, x0i, lam_re, lam_im, bre, bim, cre, cim,
                                     p["ssm_d"][slot].reshape(1, d), p["ssm_w_glu"], slot)
                new_re[gi].append(fr)
                new_im[gi].append(fi)
            final_w = p["final_norm_w"].reshape(1, d) if layer == depth - 1 else None
            xs = _moe(xs, layer_mods, nw2, p["moe_w_router"][slot], p["moe_b_router"][slot],
                      p["moe_w_in"], p["moe_w_out"], slot, final_w)
    if depth % 2 == 1:
        xs = [_final_norm(x3, p["final_norm_w"].reshape(1, d)) for x3 in xs]
    return xs, new_hgrn, [jnp.stack(r) for r in new_re], [jnp.stack(r) for r in new_im]


def kernel(x_prompt, x_sample, state_hgrn, state_ssm_re, state_ssm_im, c_prompt, c_sample, norm_w, ada_w, ada_b,
           hgrn_w_in, hgrn_lb, hgrn_gnorm, hgrn_w_out, ssm_a_re, ssm_a_im, ssm_log_dt, ssm_b_re, ssm_b_im,
           ssm_c_re, ssm_c_im, ssm_d, ssm_w_glu, ffn_w_in, ffn_w_out, moe_w_router, moe_b_router, moe_w_in,
           moe_w_out, final_norm_w):
    bp, lp, d = x_prompt.shape
    bs, ls, _ = x_sample.shape
    n_ssm, g, n_p = ssm_a_re.shape

    pr = jax.nn.softmax(hgrn_lb, axis=0)
    cs = jnp.cumsum(pr, axis=0)
    lb = cs - cs[0:1]
    p = {
        "norm_w": norm_w,
        "log_lb": jnp.log(lb)[:, None, :],
        "log1m_lb": jnp.log1p(-lb)[:, None, :],
        "hgrn_w_in": hgrn_w_in.astype(BF16),
        "hgrn_gnorm": hgrn_gnorm,
        "hgrn_w_out": hgrn_w_out.astype(BF16),
        "s5": [_s5_discretize(ssm_a_re[i], ssm_a_im[i], ssm_log_dt[i], ssm_b_re[i], ssm_b_im[i],
                              ssm_c_re[i], ssm_c_im[i]) for i in range(n_ssm)],
        "ssm_d": ssm_d,
        "ssm_w_glu": ssm_w_glu.astype(BF16),
        "ffn_w_in": ffn_w_in.astype(BF16),
        "ffn_w_out": ffn_w_out.astype(BF16),
        "moe_w_router": moe_w_router,
        "moe_b_router": moe_b_router,
        "moe_w_in": moe_w_in.astype(BF16),
        "moe_w_out": moe_w_out.astype(BF16),
        "final_norm_w": final_norm_w,
    }

    mods = _modulation(jnp.concatenate([c_prompt, c_sample], axis=0), ada_w, ada_b)
    mods_p = mods[:, :bp]
    mods_s = mods[:, bp:]

    (yp, ys), (hgrn_p, hgrn_s), (re_p, re_s), (im_p, im_s) = _trunk(
        [x_prompt, x_sample], [mods_p, mods_s], [None, state_hgrn], [None, state_ssm_re], [None, state_ssm_im], p)

    def ssm_shape(a, b):
        return a.reshape(n_ssm, b, g, n_p)

    return (yp, ys, hgrn_p, ssm_shape(re_p, bp), ssm_shape(im_p, bp),
            hgrn_s, ssm_shape(re_s, bs), ssm_shape(im_s, bs))
```

```python
import functools
import math

import jax
import jax.numpy as jnp
from jax import lax
from jax.experimental import pallas as pl
from jax.experimental.pallas import tpu as pltpu

F32 = jnp.float32
BF16 = jnp.bfloat16
HIGHEST = lax.Precision.HIGHEST

NORM_EPS = 1e-6
LOG2_E = 1.4426950408889634
LANES = 128
SUBLANES = 8
BF16_ROWS = 16
ROW_TILE = 512
HEAD_DIM = 128
HGRN_IN_COLS = 256
GLA_CHUNK = 64
GLA_HEADS_PER_STEP = 64
SSM_GROUP = 16
SSM_LANE_CHUNK = 1024
MOE_TILE = 768
VMEM_LIMIT = 56 * 1024 * 1024


def _params(*sem):
    return pltpu.CompilerParams(dimension_semantics=sem, vmem_limit_bytes=VMEM_LIMIT)


def _sigmoid(x):
    return 1.0 / (1.0 + jnp.exp(-x))


def _silu(x):
    return x * _sigmoid(x)


def _dot(a, b):
    return jnp.dot(a, b, preferred_element_type=F32)


def _tile_geometry(b, l):
    tl = min(l, ROW_TILE)
    bb = max(1, min(b, ROW_TILE // tl))
    assert l % tl == 0 and b % bb == 0
    return bb, tl


def _mod_rows(m_ref, b0, bb, tl):
    if bb == 1:
        return m_ref[pl.ds(b0, 1), :]
    if bb % SUBLANES == 0:
        b0 = pl.multiple_of(b0, SUBLANES)
    m = m_ref[pl.ds(b0, bb), :]
    return jnp.concatenate([jnp.broadcast_to(m[i : i + 1, :], (tl, m.shape[1])) for i in range(bb)], axis=0)


def _norm_mod(x, nw, sc, sh):
    ms = jnp.mean(x * x, axis=-1, keepdims=True)
    return x * lax.rsqrt(ms + NORM_EPS) * nw * (1.0 + sc) + sh


def _row_specs(b, l, d):
    bb, tl = _tile_geometry(b, l)
    per_batch = l // tl
    x_spec = pl.BlockSpec((bb, tl, d), lambda i, *_: (i // per_batch, i % per_batch, 0))

    def mod_spec(k):
        return pl.BlockSpec((b, d), lambda i, *_: (0, k))

    return x_spec, mod_spec


def _tile(ref):
    bb, tl, d = ref.shape
    return ref[...].reshape(bb * tl, d)


def _mod_kernel(c_ref, w_ref, b_ref, o_ref):
    c = c_ref[...]
    o_ref[0] = _dot(_silu(c).astype(BF16), w_ref[0].astype(BF16)) + b_ref[0]


def _modulation(c_all, ada_w, ada_b):
    depth, d, n = ada_w.shape
    rows = c_all.shape[0]
    tn = 1536
    assert n % tn == 0
    return pl.pallas_call(
        _mod_kernel,
        grid=(depth, n // tn),
        in_specs=[
            pl.BlockSpec((rows, d), lambda l, j: (0, 0)),
            pl.BlockSpec((1, d, tn), lambda l, j: (l, 0, j)),
            pl.BlockSpec((1, 1, tn), lambda l, j: (l, 0, j)),
        ],
        out_specs=pl.BlockSpec((1, rows, tn), lambda l, j: (l, 0, j)),
        out_shape=jax.ShapeDtypeStruct((depth, rows, n), F32),
        compiler_params=_params("arbitrary", "arbitrary"),
        name="adaln_modulation",
    )(c_all, ada_w, ada_b.reshape(depth, 1, n))


def _hgrn_in_kernel(x_ref, nw_ref, sh_ref, sc_ref, w_ref, la_ref, lc_ref, q_ref, lf_ref, v_ref, g_ref,
                    *, bb, tl, l_total):
    d = x_ref.shape[-1]
    rows = bb * tl
    b0 = (pl.program_id(0) * rows) // l_total
    h = _norm_mod(_tile(x_ref), nw_ref[...], _mod_rows(sc_ref, b0, bb, tl), _mod_rows(sh_ref, b0, bb, tl))
    h = h.astype(BF16)

    def put(ref, cs, val):
        ref[:, :, cs] = val.astype(ref.dtype).reshape(bb, tl, HGRN_IN_COLS)

    for c0 in range(0, d, HGRN_IN_COLS):
        cs = slice(c0, c0 + HGRN_IN_COLS)
        put(q_ref, cs, _silu(_dot(h, w_ref[:, c0 : c0 + HGRN_IN_COLS])))
        f = _dot(h, w_ref[:, d + c0 : d + c0 + HGRN_IN_COLS])
        log_sig = jnp.minimum(f, 0.0) - jnp.log1p(jnp.exp(-jnp.abs(f)))
        c = lc_ref[:, cs] + log_sig
        a = la_ref[:, cs]
        put(lf_ref, cs, jnp.maximum(a, c) + jnp.log1p(jnp.exp(-jnp.abs(a - c))))
        put(v_ref, cs, _dot(h, w_ref[:, 2 * d + c0 : 2 * d + c0 + HGRN_IN_COLS]))
        put(g_ref, cs, _silu(_dot(h, w_ref[:, 3 * d + c0 : 3 * d + c0 + HGRN_IN_COLS])))


def _hgrn_in(x3, mod, nw, w_in, slot, log_lb, log1m_lb):
    b, l, d = x3.shape
    bb, tl = _tile_geometry(b, l)
    x_spec, mod_spec = _row_specs(b, l, d)
    row = pl.BlockSpec((1, d), lambda i: (0, 0))
    act = BF16 if l % BF16_ROWS == 0 else F32
    return pl.pallas_call(
        functools.partial(_hgrn_in_kernel, bb=bb, tl=tl, l_total=l),
        grid=(b * l // (bb * tl),),
        in_specs=[x_spec, row, mod_spec(0), mod_spec(1),
                  pl.BlockSpec((None, d, 4 * d), lambda i: (slot, 0, 0)), row, row],
        out_specs=[x_spec, x_spec, x_spec, x_spec],
        out_shape=[jax.ShapeDtypeStruct((b, l, d), act), jax.ShapeDtypeStruct((b, l, d), F32),
                   jax.ShapeDtypeStruct((b, l, d), act), jax.ShapeDtypeStruct((b, l, d), act)],
        compiler_params=_params("arbitrary"),
        name="hgrn_in_proj",
    )(x3, nw, mod, mod, w_in, log_lb, log1m_lb)


def _cumsum_rows(x):
    n = x.shape[0]
    row = lax.broadcasted_iota(jnp.int32, x.shape, 0)
    shift = 1
    while shift < n:
        x = x + jnp.where(row >= shift, pltpu.roll(x, shift, axis=0), 0.0)
        shift *= 2
    return x


def _gla_head(q, lf, v, g, gw, s, c):
    nb = c // SUBLANES
    lf2 = lf * LOG2_E
    b = _cumsum_rows(lf2)
    k = 1.0 - jnp.exp2(lf2)
    vb = v.astype(BF16)

    o = _dot((q * jnp.exp2(b)).astype(BF16), s.astype(BF16))

    b3 = b.reshape(nb, SUBLANES, HEAD_DIM)
    q3 = q.reshape(nb, SUBLANES, HEAD_DIM)
    k3 = k.reshape(nb, SUBLANES, HEAD_DIM)
    v3 = v.reshape(nb, SUBLANES, HEAD_DIM)
    t_in = lax.broadcasted_iota(jnp.int32, (nb, SUBLANES, HEAD_DIM), 1)
    o_diag = jnp.zeros((nb, SUBLANES, HEAD_DIM), F32)
    for s_in in range(SUBLANES):
        diff = b3 - b3[:, s_in : s_in + 1, :]
        e = jnp.exp2(jnp.where(t_in >= s_in, diff, -jnp.inf))
        w = jnp.sum(q3 * e * k3[:, s_in : s_in + 1, :], axis=-1, keepdims=True)
        o_diag = o_diag + w * v3[:, s_in : s_in + 1, :]
    o = o + o_diag.reshape(c, HEAD_DIM)

    if c > SUBLANES:
        row = lax.broadcasted_iota(jnp.int32, (c, c), 0)
        col = lax.broadcasted_iota(jnp.int32, (c, c), 1)
        t_row = lax.broadcasted_iota(jnp.int32, (c, HEAD_DIM), 0)
        a = jnp.zeros((c, c), F32)
        half = SUBLANES
        while half < c:
            span = 2 * half
            m = b.reshape(c // span, span, HEAD_DIM)[:, half - 1 : half, :]
            m = jnp.broadcast_to(m, (c // span, span, HEAD_DIM)).reshape(c, HEAD_DIM)
            right = (t_row & half) != 0
            qh = q * jnp.exp2(jnp.where(right, b - m, -jnp.inf))
            kh = k * jnp.exp2(jnp.where(right, -jnp.inf, m - b))
            sc = lax.dot_general(qh.astype(BF16), kh.astype(BF16), (((1,), (1,)), ((), ())),
                                 preferred_element_type=F32)
            shift = int(math.log2(span))
            a = a + jnp.where((row >> shift) == (col >> shift), sc, 0.0)
            half = span
        o = o + _dot(a.astype(BF16), vb)

    b_last = b[c - 1 : c, :]
    kd = k * jnp.exp2(b_last - b)
    pad = HEAD_DIM - c
    x = jnp.concatenate([kd, jnp.broadcast_to(jnp.exp2(b_last), (pad, HEAD_DIM))], axis=0)
    xt = x.T
    v_pad = jnp.concatenate([v, jnp.zeros((pad, HEAD_DIM), F32)], axis=0).astype(BF16)
    s_new = xt[:, c : c + 1] * s + _dot(xt.astype(BF16), v_pad)

    ms = jnp.mean(o * o, axis=-1, keepdims=True)
    return o * lax.rsqrt(ms + NORM_EPS) * gw * g, s_new


def _gla_kernel(*refs, c, heads, n_chunks, has_s0):
    q_ref, lf_ref, v_ref, g_ref, gw_ref = refs[:5]
    s0_ref = refs[5] if has_s0 else None
    o_ref, sf_ref, s_scr = refs[-3:]
    step = pl.program_id(1)

    @pl.when(step == 0)
    def _():
        if has_s0:
            s_scr[...] = s0_ref[...]
        else:
            s_scr[...] = jnp.zeros_like(s_scr)

    gw = gw_ref[...]
    for bi in range(q_ref.shape[0]):
        for h in range(heads):
            sl = slice(h * HEAD_DIM, (h + 1) * HEAD_DIM)
            out, s_new = _gla_head(q_ref[bi, :, sl].astype(F32), lf_ref[bi, :, sl], v_ref[bi, :, sl].astype(F32),
                                   g_ref[bi, :, sl].astype(F32), gw, s_scr[bi, h], c)
            o_ref[bi, :, sl] = out.astype(o_ref.dtype)
            s_scr[bi, h] = s_new

    @pl.when(step == n_chunks - 1)
    def _():
        sf_ref[...] = s_scr[...]


def _gla(q3, lf3, v3, g3, gnorm_w, s0_all, s_prev, slot, n_slots):
    b, l, d = q3.shape
    heads = d // HEAD_DIM
    c = math.gcd(l, GLA_CHUNK)
    n_chunks = l // c
    bt = max(1, min(b, GLA_HEADS_PER_STEP // heads // max(1, c // 32)))
    assert b % bt == 0
    has_s0 = s0_all is not None
    blk = pl.BlockSpec((bt, c, d), lambda i, j: (i, j, 0))
    s_spec = pl.BlockSpec((None, bt, heads, HEAD_DIM, HEAD_DIM), lambda i, j: (slot, i, 0, 0, 0))
    args = [q3, lf3, v3, g3, gnorm_w]
    in_specs = [blk, blk, blk, blk, pl.BlockSpec((1, HEAD_DIM), lambda i, j: (0, 0))]
    if has_s0:
        args.append(s0_all)
        in_specs.append(s_spec)
    if s_prev is not None:
        args.append(s_prev)
        in_specs.append(pl.BlockSpec(memory_space=pl.ANY))
    o, s_fin = pl.pallas_call(
        functools.partial(_gla_kernel, c=c, heads=heads, n_chunks=n_chunks, has_s0=has_s0),
        grid=(b // bt, n_chunks),
        in_specs=in_specs,
        out_specs=[blk, s_spec],
        out_shape=[jax.ShapeDtypeStruct((b, l, d), q3.dtype),
                   jax.ShapeDtypeStruct((n_slots, b, heads, HEAD_DIM, HEAD_DIM), F32)],
        input_output_aliases={len(args) - 1: 1} if s_prev is not None else {},
        scratch_shapes=[pltpu.VMEM((bt, heads, HEAD_DIM, HEAD_DIM), F32)],
        compiler_params=_params("arbitrary", "arbitrary"),
        name="hgrn_gla_scan",
    )(*args)
    return o, s_fin


def _ff_tile(d_ff):
    best = LANES
    for t in range(LANES, 1408 + 1, LANES):
        if d_ff % t == 0:
            best = t
    return best


def _ffn_kernel(a_ref, wp_ref, g1_ref, x_ref, nw_ref, sh_ref, sc_ref, gt_ref, wg_ref, wu_ref, wo_ref, o_ref,
                x1_scr, h_scr, acc_scr, *, nf, bb, tl, l_total):
    rows = bb * tl
    b0 = (pl.program_id(0) * rows) // l_total
    j = pl.program_id(1)

    @pl.when(j == 0)
    def _():
        x1 = _tile(x_ref) + _mod_rows(g1_ref, b0, bb, tl) * _dot(_tile(a_ref).astype(BF16), wp_ref[...])
        x1_scr[...] = x1
        h = _norm_mod(x1, nw_ref[...], _mod_rows(sc_ref, b0, bb, tl), _mod_rows(sh_ref, b0, bb, tl))
        h_scr[...] = h.astype(BF16)
        acc_scr[...] = jnp.zeros_like(acc_scr)

    h = h_scr[...]
    act = (_silu(_dot(h, wg_ref[...])) * _dot(h, wu_ref[...])).astype(BF16)
    acc_scr[...] += _dot(act, wo_ref[...])

    @pl.when(j == nf - 1)
    def _():
        o_ref[...] = (x1_scr[...] + _mod_rows(gt_ref, b0, bb, tl) * acc_scr[...]).reshape(o_ref.shape)


def _proj_ffn(a3, w_proj, x3, mod, nw, w_in, w_out, slot):
    b, l, d = x3.shape
    k = a3.shape[-1]
    assert k == d
    d_ff = w_out.shape[-2]
    tf = _ff_tile(d_ff)
    nf = d_ff // tf
    bb, tl = _tile_geometry(b, l)
    rows = bb * tl
    x_spec, mod_spec = _row_specs(b, l, d)
    return pl.pallas_call(
        functools.partial(_ffn_kernel, nf=nf, bb=bb, tl=tl, l_total=l),
        grid=(b * l // rows, nf),
        in_specs=[
            x_spec, pl.BlockSpec((None, k, d), lambda i, j: (slot, 0, 0)),
            mod_spec(2),
            x_spec, pl.BlockSpec((1, d), lambda i, j: (0, 0)), mod_spec(3), mod_spec(4), mod_spec(5),
            pl.BlockSpec((None, d, tf), lambda i, j: (slot, 0, j)),
            pl.BlockSpec((None, d, tf), lambda i, j: (slot, 0, nf + j)),
            pl.BlockSpec((None, tf, d), lambda i, j: (slot, j, 0)),
        ],
        out_specs=x_spec,
        out_shape=jax.ShapeDtypeStruct((b, l, d), F32),
        scratch_shapes=[pltpu.VMEM((rows, d), F32), pltpu.VMEM((rows, d), BF16), pltpu.VMEM((rows, d), F32)],
        compiler_params=_params("arbitrary", "arbitrary"),
        name="proj_dense_swiglu",
    )(a3, w_proj, mod, x3, nw, mod, mod, mod, w_in, w_in, w_out)


def _route_kernel(x_ref, nw_ref, sh_ref, sc_ref, wr_ref, br_ref, cnt0_ref, h_ref, pos_ref, prob_ref, cnt_ref,
                  cnt_scr, *, cap, n_steps, bb, tl, l_total):
    rows = bb * tl
    step = pl.program_id(0)
    b0 = (step * rows) // l_total

    @pl.when(step == 0)
    def _():
        cnt_scr[...] = cnt0_ref[...]

    h = _norm_mod(_tile(x_ref), nw_ref[...], _mod_rows(sc_ref, b0, bb, tl), _mod_rows(sh_ref, b0, bb, tl))
    h_ref[...] = h
    logits = jnp.dot(h, wr_ref[...], preferred_element_type=F32, precision=HIGHEST) + br_ref[...]
    lane = lax.broadcasted_iota(jnp.int32, (rows, LANES), 1)
    m0 = jnp.max(logits, axis=-1, keepdims=True)
    i0 = jnp.min(jnp.where(logits == m0, lane, LANES), axis=-1, keepdims=True)
    rest = jnp.where(lane == i0, -jnp.inf, logits)
    m1 = jnp.max(rest, axis=-1, keepdims=True)
    i1 = jnp.min(jnp.where(rest == m1, lane, LANES), axis=-1, keepdims=True)
    e1 = jnp.exp(m1 - m0)
    p0 = 1.0 / (1.0 + e1)
    p1 = e1 / (1.0 + e1)
    hit0 = lane == i0
    hit1 = lane == i1
    onehot = jnp.where(hit0 | hit1, 1.0, 0.0)
    r_i = lax.broadcasted_iota(jnp.int32, (rows, rows), 0)
    c_i = lax.broadcasted_iota(jnp.int32, (rows, rows), 1)
    before = _dot((r_i > c_i).astype(BF16), onehot.astype(BF16)) + cnt_scr[...]
    rank0 = jnp.sum(jnp.where(hit0, before, 0.0), axis=-1, keepdims=True).astype(jnp.int32)
    rank1 = jnp.sum(jnp.where(hit1, before, 0.0), axis=-1, keepdims=True).astype(jnp.int32)
    pos_ref[...] = jnp.where(lane == 0, i0 * cap + rank0, jnp.where(lane == 1, i1 * cap + rank1, 0))
    prob_ref[...] = jnp.where(lane == 0, p0, jnp.where(lane == 1, p1, 0.0))
    cnt_scr[...] += jnp.sum(onehot, axis=0, keepdims=True)

    @pl.when(step == n_steps - 1)
    def _():
        cnt_ref[...] = cnt_scr[...]


def _route(x3, mod, nw, wr, br, cnt0, cap):
    b, l, d = x3.shape
    t = b * l
    bb, tl = _tile_geometry(b, l)
    rows = bb * tl
    n_steps = t // rows
    x_spec, mod_spec = _row_specs(b, l, d)
    row = pl.BlockSpec((1, LANES), lambda i: (0, 0))
    lanes_spec = pl.BlockSpec((rows, LANES), lambda i: (i, 0))
    return pl.pallas_call(
        functools.partial(_route_kernel, cap=cap, n_steps=n_steps, bb=bb, tl=tl, l_total=l),
        grid=(n_steps,),
        in_specs=[x_spec, pl.BlockSpec((1, d), lambda i: (0, 0)), mod_spec(3), mod_spec(4),
                  pl.BlockSpec((d, LANES), lambda i: (0, 0)), row, row],
        out_specs=[pl.BlockSpec((rows, d), lambda i: (i, 0)), lanes_spec, lanes_spec, row],
        out_shape=[jax.ShapeDtypeStruct((t, d), F32), jax.ShapeDtypeStruct((t, LANES), jnp.int32),
                   jax.ShapeDtypeStruct((t, LANES), F32), jax.ShapeDtypeStruct((1, LANES), F32)],
        scratch_shapes=[pltpu.VMEM((1, LANES), F32)],
        compiler_params=_params("arbitrary"),
        name="moe_route",
    )(x3, nw, mod, mod, wr, br, cnt0)


def _row_copy(src, src_row, dst, dst_row, sem):
    return pltpu.make_async_copy(src.at[pl.ds(src_row, 1)], dst.at[pl.ds(dst_row, 1)], sem)


def _scatter_kernel(cnt_ref, pos_ref, h_ref, *rest, rows, cap, n_steps, fill):
    if fill:
        _, xg_ref, zero_blk, sem = rest
    else:
        xg_ref, zero_blk, sem = rest

    for r in range(rows):
        _row_copy(h_ref, r, xg_ref, pos_ref[0, 0, 2 * r], sem).start(priority=0)
        _row_copy(h_ref, r, xg_ref, pos_ref[0, 0, 2 * r + 1], sem).start(priority=1)
    for _ in range(2):
        pltpu.make_async_copy(h_ref, xg_ref.at[pl.ds(0, rows)], sem).wait()

    if fill:
        @pl.when(pl.program_id(0) == n_steps - 1)
        def _():
            zero_blk[...] = jnp.zeros_like(zero_blk)
            for e in range(cnt_ref.shape[0]):
                cnt = cnt_ref[e]
                n_pad = lax.rem(MOE_TILE - lax.rem(cnt, MOE_TILE), MOE_TILE)
                n_single = lax.rem(SUBLANES - lax.rem(cnt, SUBLANES), SUBLANES)
                n_blocks = (n_pad - n_single) // SUBLANES
                base = e * cap + cnt

                def single(r, wait, base=base):
                    cp = _row_copy(zero_blk, 0, xg_ref, base + r, sem)
                    cp.wait() if wait else cp.start()

                def block(k, wait, base=base + n_single):
                    dst = xg_ref.at[pl.ds(pl.multiple_of(base + k * SUBLANES, SUBLANES), SUBLANES)]
                    cp = pltpu.make_async_copy(zero_blk, dst, sem)
                    cp.wait() if wait else cp.start()

                for wait in (False, True):
                    lax.fori_loop(0, n_single, lambda r, c, w=wait: single(r, w) or c, 0)
                    lax.fori_loop(0, n_blocks, lambda k, c, w=wait: block(k, w) or c, 0)


def _scatter(counts, pos, h2, xg, n_slots, cap):
    t, d = h2.shape
    rows = min(ROW_TILE, t)
    n_steps = t // rows
    fill = xg is not None
    in_specs = [pl.BlockSpec((1, 1, 2 * rows), lambda i, c: (i, 0, 0), memory_space=pltpu.SMEM),
                pl.BlockSpec((rows, d), lambda i, c: (i, 0))]
    args = [counts, pos, h2]
    if fill:
        in_specs.append(pl.BlockSpec(memory_space=pl.ANY))
        args.append(xg)
    return pl.pallas_call(
        functools.partial(_scatter_kernel, rows=rows, cap=cap, n_steps=n_steps, fill=fill),
        grid_spec=pltpu.PrefetchScalarGridSpec(
            num_scalar_prefetch=1, grid=(n_steps,), in_specs=in_specs,
            out_specs=pl.BlockSpec(memory_space=pl.ANY),
            scratch_shapes=[pltpu.VMEM((SUBLANES, d), F32), pltpu.SemaphoreType.DMA(())]),
        out_shape=jax.ShapeDtypeStruct((n_slots, d), F32),
        input_output_aliases={3: 0} if fill else {},
        compiler_params=_params("arbitrary"),
        name="moe_scatter",
    )(*args)


def _gmm_kernel(blk_ref, exp_ref, used_ref, x_ref, wg_ref, wu_ref, wo_ref, o_ref, xb_scr):
    i = pl.program_id(0)
    j = pl.program_id(1)

    @pl.when(i < used_ref[0])
    def _():
        @pl.when(j == 0)
        def _():
            xb_scr[...] = x_ref[...].astype(BF16)

        h = xb_scr[...]
        act = (_silu(_dot(h, wg_ref[...])) * _dot(h, wu_ref[...])).astype(BF16)
        y = _dot(act, wo_ref[...])

        @pl.when(j == 0)
        def _():
            o_ref[...] = y

        @pl.when(j > 0)
        def _():
            o_ref[...] += y


def _gmm(tile_blk, tile_exp, n_used, xg, w_in, w_out, slot):
    n_slots, d = xg.shape
    d_ff = w_out.shape[-2]
    tf = _ff_tile(d_ff)
    nf = d_ff // tf
    n_tiles = tile_blk.shape[0]

    def ff(i, j, used):
        return jnp.where(i < used[0], j, nf - 1)

    row_spec = pl.BlockSpec((MOE_TILE, d), lambda i, j, blk, ex, used: (blk[i], 0))
    return pl.pallas_call(
        _gmm_kernel,
        grid_spec=pltpu.PrefetchScalarGridSpec(
            num_scalar_prefetch=3, grid=(n_tiles, nf),
            in_specs=[
                row_spec,
                pl.BlockSpec((None, None, d, tf), lambda i, j, blk, ex, used: (slot, ex[i], 0, ff(i, j, used))),
                pl.BlockSpec((None, None, d, tf),
                             lambda i, j, blk, ex, used: (slot, ex[i], 0, nf + ff(i, j, used))),
                pl.BlockSpec((None, None, tf, d), lambda i, j, blk, ex, used: (slot, ex[i], ff(i, j, used), 0)),
            ],
            out_specs=row_spec,
            scratch_shapes=[pltpu.VMEM((MOE_TILE, d), BF16)]),
        out_shape=jax.ShapeDtypeStruct((n_slots, d), F32),
        compiler_params=_params("arbitrary", "arbitrary"),
        name="moe_grouped_swiglu",
    )(tile_blk, tile_exp, n_used, xg, w_in, w_in, w_out)


def _combine_kernel(pos_ref, x_ref, gt_ref, prob_ref, y_ref, *rest, bb, tl, l_total, final):
    fw_ref = rest[0] if final else None
    o_ref, buf0, buf1, sem = rest[-4:]
    rows = bb * tl
    b0 = (pl.program_id(0) * rows) // l_total
    for r in range(rows):
        _row_copy(y_ref, pos_ref[0, 0, 2 * r], buf0, r, sem).start(priority=0)
        _row_copy(y_ref, pos_ref[0, 0, 2 * r + 1], buf1, r, sem).start(priority=1)
    for buf in (buf0, buf1):
        pltpu.make_async_copy(y_ref.at[pl.ds(0, rows)], buf, sem).wait()
    prob = prob_ref[...]
    f = prob[:, 0:1] * buf0[...] + prob[:, 1:2] * buf1[...]
    x = _tile(x_ref) + _mod_rows(gt_ref, b0, bb, tl) * f
    if final:
        ms = jnp.mean(x * x, axis=-1, keepdims=True)
        x = x * lax.rsqrt(ms + NORM_EPS) * fw_ref[...]
    o_ref[...] = x.reshape(o_ref.shape)


def _combine(pos, x3, mod, prob, y, final_w):
    b, l, d = x3.shape
    bb, tl = _tile_geometry(b, l)
    rows = bb * tl
    x_spec, mod_spec = _row_specs(b, l, d)
    final = final_w is not None
    in_specs = [pl.BlockSpec((1, 1, 2 * rows), lambda i: (i, 0, 0), memory_space=pltpu.SMEM),
                x_spec, mod_spec(5), pl.BlockSpec((rows, LANES), lambda i: (i, 0)),
                pl.BlockSpec(memory_space=pl.ANY)]
    args = [pos, x3, mod, prob, y]
    if final:
        in_specs.append(pl.BlockSpec((1, d), lambda i: (0, 0)))
        args.append(final_w)
    return pl.pallas_call(
        functools.partial(_combine_kernel, bb=bb, tl=tl, l_total=l, final=final),
        grid=(b * l // rows,),
        in_specs=in_specs,
        out_specs=x_spec,
        out_shape=jax.ShapeDtypeStruct((b, l, d), F32),
        scratch_shapes=[pltpu.VMEM((rows, d), F32), pltpu.VMEM((rows, d), F32), pltpu.SemaphoreType.DMA(())],
        compiler_params=_params("arbitrary"),
        name="moe_combine",
    )(*args)


def _moe(xs, mods, nw, w_router, b_router, w_in, w_out, slot, final_w=None):
    d = xs[0].shape[-1]
    n_exp = w_in.shape[1]
    t_total = sum(x.shape[0] * x.shape[1] for x in xs)
    cap = -(-t_total // MOE_TILE) * MOE_TILE
    n_slots = n_exp * cap
    n_tiles = -(-(2 * t_total + n_exp * (MOE_TILE - 1)) // MOE_TILE)
    wr = jnp.zeros((d, LANES), F32).at[:, :n_exp].set(w_router)
    br = jnp.full((1, LANES), -jnp.inf, F32).at[0, :n_exp].set(b_router)

    cnt = jnp.zeros((1, LANES), F32)
    routed = []
    for x3, mod in zip(xs, mods):
        h2, pos, prob, cnt = _route(x3, mod, nw, wr, br, cnt, cap)
        rows = min(ROW_TILE, pos.shape[0])
        routed.append((h2, pos[:, :2].reshape(pos.shape[0] // rows, 1, 2 * rows), prob))
    counts = cnt[0, :n_exp].astype(jnp.int32)

    xg = None
    for h2, pos, _ in routed:
        xg = _scatter(counts, pos, h2, xg, n_slots, cap)

    tiles = (counts + (MOE_TILE - 1)) // MOE_TILE
    ends = jnp.cumsum(tiles)
    n_used = ends[-1:]
    idx = jnp.minimum(jnp.arange(n_tiles, dtype=jnp.int32), n_used - 1)
    tile_exp = jnp.sum((idx[:, None] >= ends[None, :]).astype(jnp.int32), axis=1)
    tile_blk = tile_exp * (cap // MOE_TILE) + idx - (ends - tiles)[tile_exp]
    y = _gmm(tile_blk.astype(jnp.int32), tile_exp, n_used.astype(jnp.int32), xg, w_in, w_out, slot)

    return [_combine(pos, x3, mod, prob, y, final_w) for x3, mod, (_, pos, prob) in zip(xs, mods, routed)]


def _gelu_tanh(x):
    return 0.5 * x * (1.0 + jnp.tanh(math.sqrt(2.0 / math.pi) * (x + 0.044715 * (x * x * x))))


def _s5_kernel(x_ref, nw_ref, sh_ref, sc_ref, gt_ref, bre_ref, bim_ref, cre_ref, cim_ref, lr_ref, li_ref,
               dsk_ref, wglu_ref, x0r_ref, x0i_ref, o_ref, fr_ref, fi_ref,
               u_scr, bur, bui, y_scr, sr, si, to_tm, to_bm, *, n_steps):
    bb, tc, d = x_ref.shape
    rows = bb * tc
    n_lane_chunks = bre_ref.shape[0]
    in_chunk = bre_ref.shape[1]
    half_lanes = SSM_LANE_CHUNK // 2
    b0 = pl.program_id(0) * bb
    t = pl.program_id(1)

    @pl.when(t == 0)
    def _():
        sr[...] = x0r_ref[...]
        si[...] = x0i_ref[...]
        r_i = lax.broadcasted_iota(jnp.int32, (rows, rows), 0)
        c_i = lax.broadcasted_iota(jnp.int32, (rows, rows), 1)
        bb_bits, tc_bits = bb.bit_length() - 1, tc.bit_length() - 1
        to_tm[...] = (c_i == ((r_i & (bb - 1)) << tc_bits) + (r_i >> bb_bits)).astype(BF16)
        to_bm[...] = (c_i == ((r_i & (tc - 1)) << bb_bits) + (r_i >> tc_bits)).astype(BF16)

    x = x_ref[...].reshape(rows, d)
    h = _norm_mod(x, nw_ref[...], _mod_rows(sc_ref, b0, bb, tc), _mod_rows(sh_ref, b0, bb, tc))
    h_hi = h.astype(BF16)
    h_lo = (h - h_hi.astype(F32)).astype(BF16)
    u_scr[...] = _dot(to_tm[...], h_hi) + _dot(to_tm[...], h_lo)

    for n in range(n_lane_chunks):
        u = u_scr[:, n * in_chunk : (n + 1) * in_chunk].astype(BF16)
        bur[...] = _dot(u, bre_ref[n])
        bui[...] = _dot(u, bim_ref[n])
        halves = [(slice(hf * half_lanes, (hf + 1) * half_lanes),
                   slice(n * SSM_LANE_CHUNK + hf * half_lanes, n * SSM_LANE_CHUNK + (hf + 1) * half_lanes))
                  for hf in range(2)]
        lam = [(jnp.broadcast_to(lr_ref[:, gs], (SUBLANES, half_lanes)),
                jnp.broadcast_to(li_ref[:, gs], (SUBLANES, half_lanes))) for _, gs in halves]

        def batch_tile(bt, _):
            r0 = pl.multiple_of(bt * SUBLANES, SUBLANES)

            def step(l, carry):
                row = pl.multiple_of(l * bb + r0, SUBLANES)
                new = []
                for (ls, _), (lam_r, lam_i), (xr, xi) in zip(halves, lam, carry):
                    nr = lam_r * xr - lam_i * xi + bur[pl.ds(row, SUBLANES), ls]
                    ni = lam_r * xi + lam_i * xr + bui[pl.ds(row, SUBLANES), ls]
                    bur[pl.ds(row, SUBLANES), ls] = nr
                    bui[pl.ds(row, SUBLANES), ls] = ni
                    new.append((nr, ni))
                return tuple(new)

            init = tuple((sr[pl.ds(r0, SUBLANES), gs], si[pl.ds(r0, SUBLANES), gs]) for _, gs in halves)
            final = lax.fori_loop(0, tc, step, init, unroll=8)
            for (_, gs), (xr, xi) in zip(halves, final):
                sr[pl.ds(r0, SUBLANES), gs] = xr
                si[pl.ds(r0, SUBLANES), gs] = xi
            return 0

        lax.fori_loop(0, bb // SUBLANES, batch_tile, 0)
        y_scr[:, n * in_chunk : (n + 1) * in_chunk] = (
            _dot(bur[...].astype(BF16), cre_ref[n]) - _dot(bui[...].astype(BF16), cim_ref[n]))

    y = y_scr[...] + dsk_ref[...] * u_scr[...]
    z = _dot(to_bm[...], _gelu_tanh(y).astype(BF16)).astype(BF16)
    zz = _dot(z, wglu_ref[...])
    m = zz[:, :d] * _sigmoid(zz[:, d:])
    o_ref[...] = (x + _mod_rows(gt_ref, b0, bb, tc) * m).reshape(bb, tc, d)

    @pl.when(t == n_steps - 1)
    def _():
        fr_ref[...] = sr[...]
        fi_ref[...] = si[...]


def _s5(x3, mod, nw, x0_re, x0_im, lam_re, lam_im, b_re_blk, b_im_blk, c_re_blk, c_im_blk, d_skip, w_glu, slot):
    b, l, d = x3.shape
    n_state = lam_re.shape[-1]
    if b * l <= ROW_TILE:
        bb, tc = b, l
    elif b <= ROW_TILE // SUBLANES:
        bb, tc = b, ROW_TILE // b
    else:
        tc = l
        bb = ROW_TILE // l
    assert l % tc == 0 and b % bb == 0 and bb % SUBLANES == 0 and tc % SUBLANES == 0
    assert bb & (bb - 1) == 0 and tc & (tc - 1) == 0
    n_steps = l // tc
    rows = tc * bb
    x_spec = pl.BlockSpec((bb, tc, d), lambda i, s: (i, s, 0))

    def mod_spec(k):
        return pl.BlockSpec((b, d), lambda i, s: (0, k))

    def full(a):
        nd = a.ndim
        return pl.BlockSpec(a.shape, lambda i, s: (0,) * nd)

    st_spec = pl.BlockSpec((bb, n_state), lambda i, s: (i, 0))
    out, fr, fi = pl.pallas_call(
        functools.partial(_s5_kernel, n_steps=n_steps),
        grid=(b // bb, n_steps),
        in_specs=[x_spec, full(nw), mod_spec(0), mod_spec(1), mod_spec(2), full(b_re_blk), full(b_im_blk),
                  full(c_re_blk), full(c_im_blk), full(lam_re), full(lam_im), full(d_skip),
                  pl.BlockSpec((None,) + w_glu.shape[1:], lambda i, s: (slot, 0, 0)),
                  st_spec, st_spec],
        out_specs=[x_spec, st_spec, st_spec],
        out_shape=[jax.ShapeDtypeStruct((b, l, d), F32), jax.ShapeDtypeStruct((b, n_state), F32),
                   jax.ShapeDtypeStruct((b, n_state), F32)],
        scratch_shapes=[pltpu.VMEM((rows, d), F32), pltpu.VMEM((rows, SSM_LANE_CHUNK), F32),
                        pltpu.VMEM((rows, SSM_LANE_CHUNK), F32), pltpu.VMEM((rows, d), F32),
                        pltpu.VMEM((bb, n_state), F32), pltpu.VMEM((bb, n_state), F32),
                        pltpu.VMEM((rows, rows), BF16), pltpu.VMEM((rows, rows), BF16)],
        compiler_params=_params("arbitrary", "arbitrary"),
        name="s5_mixer",
    )(x3, nw, mod, mod, mod, b_re_blk, b_im_blk, c_re_blk, c_im_blk, lam_re, lam_im, d_skip, w_glu, x0_re, x0_im)
    return out, fr, fi


def _s5_discretize(a_re, a_im, log_dt, b_re, b_im, c_re, c_im):
    g, p = a_re.shape
    dt = jnp.exp(log_dt)[:, None]
    mag = jnp.exp(a_re * dt)
    lam_re = mag * jnp.cos(a_im * dt)
    lam_im = mag * jnp.sin(a_im * dt)
    den = a_re * a_re + a_im * a_im
    coef_re = ((lam_re - 1.0) * a_re + lam_im * a_im) / den
    coef_im = (lam_im * a_re - (lam_re - 1.0) * a_im) / den
    bb_re = coef_re[..., None] * b_re - coef_im[..., None] * b_im
    bb_im = coef_re[..., None] * b_im + coef_im[..., None] * b_re
    gpb = SSM_LANE_CHUNK // p
    nblk = g // gpb
    eye = jnp.eye(gpb, dtype=F32)

    def in_map(bb):
        t = bb.reshape(nblk, gpb, p, SSM_GROUP)
        return jnp.einsum("ngpc,gh->ngchp", t, eye).reshape(nblk, gpb * SSM_GROUP, gpb * p).astype(BF16)

    def out_map(c):
        t = c.reshape(nblk, gpb, SSM_GROUP, p)
        return jnp.einsum("ngcp,gh->ngphc", t, eye).reshape(nblk, gpb * p, gpb * SSM_GROUP).astype(BF16)

    return (lam_re.reshape(1, g * p), lam_im.reshape(1, g * p), in_map(bb_re), in_map(bb_im),
            out_map(c_re), out_map(c_im))


def _final_norm_kernel(x_ref, w_ref, o_ref):
    x = x_ref[...]
    ms = jnp.mean(x * x, axis=-1, keepdims=True)
    o_ref[...] = x * lax.rsqrt(ms + NORM_EPS) * w_ref[...]


def _final_norm(x3, w):
    b, l, d = x3.shape
    bb, tl = _tile_geometry(b, l)
    x_spec, _ = _row_specs(b, l, d)
    return pl.pallas_call(
        _final_norm_kernel,
        grid=(b * l // (bb * tl),),
        in_specs=[x_spec, pl.BlockSpec((1, d), lambda i: (0, 0))],
        out_specs=x_spec,
        out_shape=jax.ShapeDtypeStruct((b, l, d), F32),
        compiler_params=_params("arbitrary"),
        name="final_norm",
    )(x3, w)


def _trunk(xs, mods, s_hgrn, s_re, s_im, p):
    depth = p["norm_w"].shape[0]
    n_groups = len(xs)
    d = xs[0].shape[-1]
    n_hgrn = (depth + 1) // 2
    new_hgrn = [None] * n_groups
    new_re = [[] for _ in xs]
    new_im = [[] for _ in xs]
    for layer in range(depth):
        slot = layer // 2
        nw1 = p["norm_w"][layer, 0].reshape(1, d)
        nw2 = p["norm_w"][layer, 1].reshape(1, d)
        layer_mods = [m[layer] for m in mods]
        if layer % 2 == 0:
            for gi in range(n_groups):
                x3, mod = xs[gi], layer_mods[gi]
                q, lf, v, g = _hgrn_in(x3, mod, nw1, p["hgrn_w_in"], slot, p["log_lb"][slot], p["log1m_lb"][slot])
                o, new_hgrn[gi] = _gla(q, lf, v, g, p["hgrn_gnorm"][slot].reshape(1, HEAD_DIM),
                                       s_hgrn[gi], new_hgrn[gi], slot, n_hgrn)
                xs[gi] = _proj_ffn(o, p["hgrn_w_out"], x3, mod, nw2, p["ffn_w_in"], p["ffn_w_out"], slot)
        else:
            lam_re, lam_im, bre, bim, cre, cim = p["s5"][slot]
            n_state = lam_re.shape[-1]
            for gi in range(n_groups):
                b = xs[gi].shape[0]
                if s_re[gi] is None:
                    x0r = x0i = jnp.zeros((b, n_state), F32)
                else:
                    x0r = s_re[gi][slot].reshape(b, n_state)
                    x0i = s_im[gi][slot].reshape(b, n_state)
                xs[gi], fr, fi = _s5(xs[gi], layer_mods[gi], nw1, x0r, x0i, lam_re, lam_im, bre, bim, cre, cim,
                                     p["ssm_d"][slot].reshape(1, d), p["ssm_w_glu"], slot)
                new_re[gi].append(fr)
                new_im[gi].append(fi)
            final_w = p["final_norm_w"].reshape(1, d) if layer == depth - 1 else None
            xs = _moe(xs, layer_mods, nw2, p["moe_w_router"][slot], p["moe_b_router"][slot],
                      p["moe_w_in"], p["moe_w_out"], slot, final_w)
    if depth % 2 == 1:
        xs = [_final_norm(x3, p["final_norm_w"].reshape(1, d)) for x3 in xs]
    return xs, new_hgrn, [jnp.stack(r) for r in new_re], [jnp.stack(r) for r in new_im]


def kernel(x_prompt, x_sample, state_hgrn, state_ssm_re, state_ssm_im, c_prompt, c_sample, norm_w, ada_w, ada_b,
           hgrn_w_in, hgrn_lb, hgrn_gnorm, hgrn_w_out, ssm_a_re, ssm_a_im, ssm_log_dt, ssm_b_re, ssm_b_im,
           ssm_c_re, ssm_c_im, ssm_d, ssm_w_glu, ffn_w_in, ffn_w_out, moe_w_router, moe_b_router, moe_w_in,
           moe_w_out, final_norm_w):
    bp, lp, d = x_prompt.shape
    bs, ls, _ = x_sample.shape
    n_ssm, g, n_p = ssm_a_re.shape

    pr = jax.nn.softmax(hgrn_lb, axis=0)
    cs = jnp.cumsum(pr, axis=0)
    lb = cs - cs[0:1]
    p = {
        "norm_w": norm_w,
        "log_lb": jnp.log(lb)[:, None, :],
        "log1m_lb": jnp.log1p(-lb)[:, None, :],
        "hgrn_w_in": hgrn_w_in.astype(BF16),
        "hgrn_gnorm": hgrn_gnorm,
        "hgrn_w_out": hgrn_w_out.astype(BF16),
        "s5": [_s5_discretize(ssm_a_re[i], ssm_a_im[i], ssm_log_dt[i], ssm_b_re[i], ssm_b_im[i],
                              ssm_c_re[i], ssm_c_im[i]) for i in range(n_ssm)],
        "ssm_d": ssm_d,
        "ssm_w_glu": ssm_w_glu.astype(BF16),
        "ffn_w_in": ffn_w_in.astype(BF16),
        "ffn_w_out": ffn_w_out.astype(BF16),
        "moe_w_router": moe_w_router,
        "moe_b_router": moe_b_router,
        "moe_w_in": moe_w_in.astype(BF16),
        "moe_w_out": moe_w_out.astype(BF16),
        "final_norm_w": final_norm_w,
    }

    mods = _modulation(jnp.concatenate([c_prompt, c_sample], axis=0), ada_w, ada_b)
    mods_p = mods[:, :bp]
    mods_s = mods[:, bp:]

    (yp, ys), (hgrn_p, hgrn_s), (re_p, re_s), (im_p, im_s) = _trunk(
        [x_prompt, x_sample], [mods_p, mods_s], [None, state_hgrn], [None, state_ssm_re], [None, state_ssm_im], p)

    def ssm_shape(a, b):
        return a.reshape(n_ssm, b, g, n_p)

    return (yp, ys, hgrn_p, ssm_shape(re_p, bp), ssm_shape(im_p, bp),
            hgrn_s, ssm_shape(re_s, bs), ssm_shape(im_s, bs))
```
